```python
import math
import jax
import jax.numpy as jnp
from jax import lax
import numpy as np

D_MODEL = 1024
BATCH = 4
SEQ = 8192
DEPTH = 1

D_MIX = D_MODEL
D_CONV = D_MIX // 2
CONV_GROUPS = 8
CONV_GROUP_DIM = D_CONV // CONV_GROUPS
CONV_K_MIX = 3
DN_HEADS = 4
DN_HEAD_DIM = (D_MIX - D_CONV) // DN_HEADS
DN_QK = DN_HEADS * DN_HEAD_DIM
DN_V = DN_HEADS * DN_HEAD_DIM
CONV_K_QKV = 4
CHUNK = 64
FFN_HIDDEN = ((8 * D_MODEL // 3 + 255) // 256) * 256
N_MOD = 6
EPS = 1e-6
P_IN = 3 * D_CONV + 2 * DN_QK + 2 * DN_V + 2 * DN_HEADS

kernel_name = 'hybrid_conv_deltanet_adaln_layer'


def rms_norm(x, gain, eps=EPS):
    x32 = x.astype(jnp.float32)
    y = x32 * lax.rsqrt(jnp.mean(x32 * x32, axis=-1, keepdims=True) + eps)
    return y.astype(x.dtype) * gain


def l2_normalize(x, eps=EPS):
    x32 = x.astype(jnp.float32)
    return (x32 * lax.rsqrt(jnp.sum(x32 * x32, axis=-1, keepdims=True) + eps)).astype(x.dtype)


def adaln(x, gain, shift, scale):
    return rms_norm(x, gain) * (1 + scale[:, None, :]) + shift[:, None, :]


def causal_depthwise_conv(u, w):
    k = w.shape[0]
    s = u.shape[1]
    up = jnp.pad(u, ((0, 0), (k - 1, 0), (0, 0)))
    out = up[:, 0:s] * w[0]
    for j in range(1, k):
        out = out + up[:, j:j + s] * w[j]
    return out


def chunked_gated_delta_rule(q, k, v, g, beta):
    out_dtype = v.dtype
    f32 = jnp.float32
    b, h, t, dk = q.shape
    dv = v.shape[-1]
    n = t // CHUNK

    def blocks(a):
        return a.astype(f32).reshape(b, h, n, CHUNK, *a.shape[3:])

    q = blocks(q) * (dk ** -0.5)
    k = blocks(k)
    v = blocks(v)
    beta = blocks(beta)
    g = jnp.cumsum(blocks(g), axis=-1)
    idx = jnp.arange(CHUNK)
    causal = idx[:, None] >= idx[None, :]
    strict = idx[:, None] > idx[None, :]
    decay = jnp.exp(jnp.where(causal, g[..., :, None] - g[..., None, :], -jnp.inf))
    k_beta = k * beta[..., None]
    lower = jnp.where(strict, jnp.einsum('bhncd,bhnsd->bhncs', k_beta, k) * decay, 0.0)
    tri = lower + jnp.eye(CHUNK, dtype=f32)
    rhs = jnp.concatenate([v * beta[..., None], k_beta * jnp.exp(g)[..., None]], axis=-1)
    sol = lax.linalg.triangular_solve(tri, rhs, left_side=True, lower=True, unit_diagonal=True)
    u = sol[..., :dv]
    w = sol[..., dv:]
    attn = jnp.where(causal, jnp.einsum('bhncd,bhnsd->bhncs', q, k) * decay, 0.0)
    g_last = g[..., -1]
    q_dec = q * jnp.exp(g)[..., None]
    k_dec = k * jnp.exp(g_last[..., None] - g)[..., None]

    def step(state, inp):
        attn_c, u_c, w_c, q_c, k_c, gl_c = inp
        v_new = u_c - jnp.einsum('bhck,bhkv->bhcv', w_c, state)
        o_c = jnp.einsum('bhck,bhkv->bhcv', q_c, state) + jnp.einsum('bhcs,bhsv->bhcv', attn_c, v_new)
        state = state * jnp.exp(gl_c)[..., None, None] + jnp.einsum('bhck,bhcv->bhkv', k_c, v_new)
        return state, o_c

    xs = tuple(jnp.moveaxis(a, 2, 0) for a in (attn, u, w, q_dec, k_dec, g_last))
    s0 = jnp.zeros((b, h, dk, dv), f32)
    _, o = lax.scan(step, s0, xs)
    o = jnp.moveaxis(o, 0, 2).reshape(b, h, t, dv)
    return o.astype(out_dtype)


def setup_inputs(seed: int = 0) -> dict:
    key = jax.random.key(seed)
    ks = jax.random.split(key, 20)
    f32 = jnp.float32
    L, D = DEPTH, D_MODEL

    def normal(k, shape, scale):
        return jax.random.normal(k, shape, f32) * scale

    dt = jnp.exp(jax.random.uniform(ks[12], (L, DN_HEADS), f32) * (math.log(0.1) - math.log(0.001)) + math.log(0.001))
    return {
        'x': normal(ks[0], (BATCH, SEQ, D), 1.0),
        'c': normal(ks[1], (BATCH, D), 1.0),
        'w_ada': normal(ks[2], (L, D, N_MOD * D), 0.5 * D ** -0.5),
        'b_ada': normal(ks[3], (L, N_MOD * D), 0.02),
        'w_ada_final': normal(ks[4], (D, 2 * D), 0.5 * D ** -0.5),
        'b_ada_final': normal(ks[5], (2 * D,), 0.02),
        'g_norm_mix': 1.0 + normal(ks[6], (L, D), 0.1),
        'g_norm_ffn': 1.0 + normal(ks[7], (L, D), 0.1),
        'g_norm_final': 1.0 + normal(ks[8], (D,), 0.1),
        'w_in': normal(ks[9], (L, D, P_IN), D ** -0.5),
        'conv_w_mix': normal(ks[10], (L, CONV_K_MIX, D_CONV), CONV_K_MIX ** -0.5),
        'conv_w_qkv': normal(ks[11], (L, CONV_K_QKV, 2 * DN_QK + DN_V), CONV_K_QKV ** -0.5),
        'a_log': jnp.log(jax.random.uniform(ks[13], (L, DN_HEADS), f32, minval=1.0, maxval=16.0)),
        'dt_bias': dt + jnp.log(-jnp.expm1(-dt)),
        'g_conv_out': 1.0 + normal(ks[14], (L, D_CONV), 0.1),
        'g_dn_out': 1.0 + normal(ks[15], (L, DN_HEAD_DIM), 0.1),
        'w_out': normal(ks[16], (L, D_MIX, D), D_MIX ** -0.5),
        'w_gate_up': normal(ks[17], (L, D, 2 * FFN_HIDDEN), D ** -0.5),
        'w_down': normal(ks[18], (L, FFN_HIDDEN, D), FFN_HIDDEN ** -0.5),
    }


def reference(x, c, w_ada, b_ada, w_ada_final, b_ada_final, g_norm_mix, g_norm_ffn, g_norm_final,
              w_in, conv_w_mix, conv_w_qkv, a_log, dt_bias, g_conv_out, g_dn_out, w_out, w_gate_up, w_down):
    b, s, _ = x.shape
    c_act = jax.nn.silu(c)
    split_at = [D_CONV, 2 * D_CONV, 3 * D_CONV,
                3 * D_CONV + DN_QK, 3 * D_CONV + 2 * DN_QK,
                3 * D_CONV + 2 * DN_QK + DN_V, 3 * D_CONV + 2 * DN_QK + 2 * DN_V,
                3 * D_CONV + 2 * DN_QK + 2 * DN_V + DN_HEADS]
    for layer in range(DEPTH):
        mod = c_act @ w_ada[layer] + b_ada[layer]
        shift_m, scale_m, gate_m, shift_f, scale_f, gate_f = jnp.split(mod, N_MOD, axis=-1)

        h = adaln(x, g_norm_mix[layer], shift_m, scale_m)
        p = h @ w_in[layer]
        gate_b, gate_c, h_conv, q, k, v, z, a, bt = jnp.split(p, split_at, axis=-1)

        ya = gate_b * causal_depthwise_conv(gate_c * h_conv, conv_w_mix[layer])
        ya = rms_norm(ya.reshape(b, s, CONV_GROUPS, CONV_GROUP_DIM),
                      g_conv_out[layer].reshape(CONV_GROUPS, CONV_GROUP_DIM)).reshape(b, s, D_CONV)

        qkv = jax.nn.silu(causal_depthwise_conv(jnp.concatenate([q, k, v], axis=-1), conv_w_qkv[layer]))
        q, k, v = jnp.split(qkv, [DN_QK, 2 * DN_QK], axis=-1)

        def heads(t):
            return t.reshape(b, s, DN_HEADS, DN_HEAD_DIM).transpose(0, 2, 1, 3)

        q = l2_normalize(heads(q))
        k = l2_normalize(heads(k))
        v = heads(v)
        beta = jax.nn.sigmoid(bt.astype(jnp.float32)).transpose(0, 2, 1)
        g_dec = (-jnp.exp(a_log[layer].astype(jnp.float32))
                 * jax.nn.softplus(a.astype(jnp.float32) + dt_bias[layer].astype(jnp.float32))).transpose(0, 2, 1)
        o = chunked_gated_delta_rule(q, k, v, g_dec, beta).transpose(0, 2, 1, 3)
        o = rms_norm(o, g_dn_out[layer]) * jax.nn.silu(z.reshape(b, s, DN_HEADS, DN_HEAD_DIM))
        yb = o.reshape(b, s, DN_V)

        y = jnp.concatenate([ya, yb], axis=-1) @ w_out[layer]
        x = x + gate_m[:, None, :] * y

        h = adaln(x, g_norm_ffn[layer], shift_f, scale_f)
        gt, up = jnp.split(h @ w_gate_up[layer], 2, axis=-1)
        x = x + gate_f[:, None, :] * ((jax.nn.silu(gt) * up) @ w_down[layer])

    shift_o, scale_o = jnp.split(c_act @ w_ada_final + b_ada_final, 2, axis=-1)
    return adaln(x, g_norm_final, shift_o, scale_o)
```

```python
import functools

import numpy as np
import jax
import jax.numpy as jnp
from jax import lax
from jax.experimental import pallas as pl
from jax.experimental.pallas import tpu as pltpu

F32 = jnp.float32
BF16 = jnp.bfloat16

D_MODEL = 1024
D_CONV = 512
CONV_GROUP_DIM = 64
DN_HEADS = 4
DN_HEAD_DIM = 128
DN_WIDTH = DN_HEADS * DN_HEAD_DIM
CHUNK = 64
FFN_HIDDEN = 2816
EPS = 1e-6

LANES = 128
SUBLANES = 8
N_MAIN = 3 * D_CONV + 4 * DN_WIDTH
PAIR = 2 * CHUNK
TOKENS_IN = 512
TOKENS_DELTA = 256
TOKENS_OUT = 512
FFN_COLS = 256
MOD_COLS = 1024
VMEM_LIMIT_BYTES = 56 * 1024 * 1024


def _dot(a, b):
    return jnp.dot(a, b, preferred_element_type=F32)


def _dot_nt(a, b):
    return lax.dot_general(a, b, (((1,), (1,)), ((), ())), preferred_element_type=F32)


def _split2(x):
    hi = x.astype(BF16)
    lo = (x - hi.astype(F32)).astype(BF16)
    return hi, lo


def _split3(x):
    hi = x.astype(BF16)
    r = x - hi.astype(F32)
    mid = r.astype(BF16)
    lo = (r - mid.astype(F32)).astype(BF16)
    return hi, mid, lo


def _dot3(a2, b2):
    return _dot(a2[0], b2[0]) + (_dot(a2[0], b2[1]) + _dot(a2[1], b2[0]))


def _exact_lhs_dot(lhs_bf16, x):
    h, m, l = _split3(x)
    return _dot(lhs_bf16, h) + (_dot(lhs_bf16, m) + _dot(lhs_bf16, l))


def _segment_sum(x, ones_bf16):
    hi, lo = _split2(x)
    return _dot(hi, ones_bf16) + _dot(lo, ones_bf16)


def _sigmoid(x):
    return 1.0 / (1.0 + jnp.exp(-x))


def _silu(x):
    return x * _sigmoid(x)


def _adaln(x, gain, shift, scale):
    ms = jnp.mean(x * x, axis=-1, keepdims=True)
    return (x * lax.rsqrt(ms + EPS)) * gain * (1.0 + scale) + shift


def _mod_kernel(c_ref, w_ref, b_ref, o_ref):
    c = c_ref[...]
    ca = _split2(_silu(c))
    o_ref[...] = _dot3(ca, _split2(w_ref[...])) + b_ref[...]


def _modulation(c_pad, w, b):
    rows, d = c_pad.shape
    n = w.shape[1]
    return pl.pallas_call(
        _mod_kernel,
        grid=(n // MOD_COLS,),
        in_specs=[
            pl.BlockSpec((rows, d), lambda j: (0, 0)),
            pl.BlockSpec((d, MOD_COLS), lambda j: (0, j)),
            pl.BlockSpec((1, MOD_COLS), lambda j: (0, j)),
        ],
        out_specs=pl.BlockSpec((rows, MOD_COLS), lambda j: (0, j)),
        out_shape=jax.ShapeDtypeStruct((rows, n), F32),
        compiler_params=pltpu.CompilerParams(dimension_semantics=("arbitrary",)),
        name="modulation",
    )(c_pad, w, b)


def _shift_rows(u, tail, j):
    r = pltpu.roll(u, j, 0)
    rt = pltpu.roll(tail, j, 0)
    rows = lax.broadcasted_iota(jnp.int32, tail.shape, 0)
    head = jnp.where(rows < j, rt, r[:SUBLANES])
    return jnp.concatenate([head, r[SUBLANES:]], axis=0)


def _causal_conv(u, tail, w):
    k = w.shape[0]
    out = u * w[k - 1:k]
    for j in range(1, k):
        out = out + _shift_rows(u, tail, j) * w[k - 1 - j:k - j]
    return out


def _mixer_in_kernel(x_ref, mod_ref, gmix_ref, w_ref, wab_ref, cwm_ref, cwq_ref, gco_ref, alog_ref, dtb_ref,
                     ones64_ref, ones128_ref,
                     ya_ref, q_ref, k_ref, v_ref, z_ref, gb_ref,
                     tail_u, tail_qkv):
    @pl.when(pl.program_id(1) == 0)
    def _():
        tail_u[...] = jnp.zeros_like(tail_u)
        tail_qkv[...] = jnp.zeros_like(tail_qkv)

    x = x_ref[0]
    mod = mod_ref[0]
    h = _adaln(x, gmix_ref[...], mod[0:1], mod[1:2]).astype(BF16)

    def proj(i, width=1):
        return _dot(h, w_ref[:, i * D_CONV:(i + width) * D_CONV])

    gate_b = proj(0)
    u = proj(1) * proj(2)
    ya = gate_b * _causal_conv(u, tail_u[...], cwm_ref[...])
    tail_u[...] = u[-SUBLANES:]
    ms = _segment_sum(ya * ya, ones64_ref[...]) * (1.0 / CONV_GROUP_DIM)
    ya_ref[0] = ((ya * lax.rsqrt(ms + EPS)) * gco_ref[...]).astype(ya_ref.dtype)

    tails = tail_qkv[...]
    cwq = cwq_ref[...]
    for i, ref in enumerate((q_ref, k_ref, v_ref)):
        cols = slice(i * DN_WIDTH, (i + 1) * DN_WIDTH)
        pre = proj(3 + i)
        t = _silu(_causal_conv(pre, tails[:, cols], cwq[:, cols]))
        tail_qkv[:, cols] = pre[-SUBLANES:]
        if i < 2:
            ss = _segment_sum(t * t, ones128_ref[...])
            t = t * lax.rsqrt(ss + EPS)
        if i == 0:
            t = t * (DN_HEAD_DIM ** -0.5)
        ref[0] = t
    z_ref[0] = proj(6)

    ab = _dot(h, wab_ref[...])
    sp_in = ab + dtb_ref[...]
    softplus = jnp.maximum(sp_in, 0.0) + jnp.log1p(jnp.exp(-jnp.abs(sp_in)))
    g = -jnp.exp(alog_ref[...]) * softplus
    lane = lax.broadcasted_iota(jnp.int32, ab.shape, 1)
    gb_ref[0] = jnp.where(lane < DN_HEADS, g, _sigmoid(ab))


def _mixer_in(x, mod, g_mix, w_main, w_ab, cw_mix, cw_qkv, g_conv_out, alog_row, dtb_row, ones64, ones128):
    b, s, d = x.shape
    tm = TOKENS_IN
    const = lambda *shape: pl.BlockSpec(shape, lambda bi, si: (0,) * len(shape))
    resident = lambda *shape: pl.BlockSpec(shape, lambda bi, si: (0,) * len(shape), pipeline_mode=pl.Buffered(1))
    tok = lambda width: pl.BlockSpec((1, tm, width), lambda bi, si: (bi, si, 0))
    return pl.pallas_call(
        _mixer_in_kernel,
        grid=(b, s // tm),
        in_specs=[
            tok(d),
            pl.BlockSpec((1, SUBLANES, d), lambda bi, si: (bi, 0, 0)),
            const(1, d),
            resident(d, N_MAIN),
            resident(d, LANES),
            const(3, D_CONV),
            const(4, 3 * DN_WIDTH),
            const(1, D_CONV),
            const(1, LANES),
            const(1, LANES),
            resident(D_CONV, D_CONV),
            resident(DN_WIDTH, DN_WIDTH),
        ],
        out_specs=[tok(D_CONV), tok(DN_WIDTH), tok(DN_WIDTH), tok(DN_WIDTH), tok(DN_WIDTH), tok(LANES)],
        out_shape=[
            jax.ShapeDtypeStruct((b, s, D_CONV), BF16),
            jax.ShapeDtypeStruct((b, s, DN_WIDTH), F32),
            jax.ShapeDtypeStruct((b, s, DN_WIDTH), F32),
            jax.ShapeDtypeStruct((b, s, DN_WIDTH), F32),
            jax.ShapeDtypeStruct((b, s, DN_WIDTH), F32),
            jax.ShapeDtypeStruct((b, s, LANES), F32),
        ],
        scratch_shapes=[pltpu.VMEM((SUBLANES, D_CONV), F32), pltpu.VMEM((SUBLANES, 3 * DN_WIDTH), F32)],
        compiler_params=pltpu.CompilerParams(dimension_semantics=("arbitrary", "arbitrary"),
                                             vmem_limit_bytes=VMEM_LIMIT_BYTES),
        name="mixer_in",
    )(x, mod, g_mix, w_main, w_ab, cw_mix, cw_qkv, g_conv_out, alog_row, dtb_row, ones64, ones128)


N_LEVELS = 6
MASK_CAUSAL = 0
MASK_STRICT = 1
MASK_LEVEL0 = 2


def _pair_masks():
    i = np.arange(PAIR)[:, None]
    j = np.arange(PAIR)[None, :]
    same_chunk = (i // CHUNK) == (j // CHUNK)
    out = [same_chunk & (j <= i), same_chunk & (j < i)]
    for l in range(N_LEVELS):
        s = 1 << l
        out.append((j < i) & ((i // s) != (j // s)) & ((i // (2 * s)) == (j // (2 * s))))
    return np.stack(out).astype(np.float32)


def _unit_lower_inverse(low, masks_ref, eye):
    t = eye - low * masks_ref[MASK_LEVEL0]
    for l in range(1, N_LEVELS):
        t2 = _split2(t)
        x = _dot3(t2, _split2(low * masks_ref[MASK_LEVEL0 + l]))
        t = t - _dot3(_split2(x), t2)
    return t


def _delta_kernel(q_ref, k_ref, v_ref, gb_ref, masks_ref, o_ref, state_ref):
    @pl.when(pl.program_id(1) == 0)
    def _():
        state_ref[...] = jnp.zeros_like(state_ref)

    n_pairs = TOKENS_DELTA // PAIR
    causal = masks_ref[MASK_CAUSAL]
    strict = masks_ref[MASK_STRICT]
    causal_b = causal.astype(BF16)
    eye = causal - strict
    lane = lax.broadcasted_iota(jnp.int32, (PAIR, PAIR), 1)

    prep = []
    for hd in range(DN_HEADS):
        cols = slice(hd * DN_HEAD_DIM, (hd + 1) * DN_HEAD_DIM)
        per_pair = []
        for p in range(n_pairs):
            rows = slice(p * PAIR, (p + 1) * PAIR)
            q = q_ref[0, rows, cols]
            k = k_ref[0, rows, cols]
            v = v_ref[0, rows, cols]
            g_col = gb_ref[0, rows, hd:hd + 1]
            beta = jnp.broadcast_to(gb_ref[0, rows, DN_HEADS + hd:DN_HEADS + hd + 1], (PAIR, DN_HEAD_DIM))
            diff = _exact_lhs_dot(causal_b, g_col * strict)
            gc = _exact_lhs_dot(causal_b, jnp.broadcast_to(g_col, (PAIR, DN_HEAD_DIM)))
            g_last = jnp.concatenate(
                [jnp.broadcast_to(gc[e * CHUNK + CHUNK - 1:(e + 1) * CHUNK], (CHUNK, DN_HEAD_DIM)) for e in range(2)],
                axis=0)
            decay = jnp.exp(diff)
            kb = k * beta
            k16 = k.astype(BF16)
            kk = _dot_nt(kb.astype(BF16), k16)
            qk = _dot_nt(q.astype(BF16), k16)
            low = jnp.where(strict > 0, kk * decay, 0.0)
            attn = jnp.where(causal > 0, qk * decay, 0.0)
            t2 = _split2(_unit_lower_inverse(low, masks_ref, eye))
            e_gc = jnp.exp(gc)
            u = _dot3(t2, _split2(v * beta))
            w = _dot3(t2, _split2(kb * e_gc))
            q_dec = q * e_gc
            k_dec_t = (k * jnp.exp(g_last - gc)).T
            per_pair.append(dict(attn=attn.astype(BF16), u=u, w=w.astype(BF16), q_dec=q_dec.astype(BF16),
                                 k_dec_t=k_dec_t, gc=gc))
        prep.append(per_pair)

    states = [state_ref[hd] for hd in range(DN_HEADS)]
    for p in range(n_pairs):
        v_new_first = [None] * DN_HEADS
        for e in range(2):
            rows = slice(e * CHUNK, (e + 1) * CHUNK)
            for hd in range(DN_HEADS):
                d = prep[hd][p]
                s16 = states[hd].astype(BF16)
                wq = jnp.concatenate([d["w"][rows], d["q_dec"][rows]], axis=0)
                ws_qs = _dot(wq, s16)
                v_new = d["u"][rows] - ws_qs[:CHUNK]
                if e == 0:
                    v_new_first[hd] = v_new
                    v_pair = jnp.concatenate([v_new, d["u"][CHUNK:]], axis=0)
                else:
                    v_pair = jnp.concatenate([v_new_first[hd], v_new], axis=0)
                v16 = v_pair.astype(BF16)
                o = ws_qs[CHUNK:] + _dot(d["attn"][rows], v16)
                in_chunk = (lane >= e * CHUNK) & (lane < (e + 1) * CHUNK)
                k_t = jnp.where(in_chunk, d["k_dec_t"], 0.0).astype(BF16)
                g_end = d["gc"][e * CHUNK + CHUNK - 1:(e + 1) * CHUNK]
                states[hd] = states[hd] * jnp.exp(g_end) + _dot(k_t, v16)
                o_ref[0, p * PAIR + e * CHUNK:p * PAIR + (e + 1) * CHUNK,
                      hd * DN_HEAD_DIM:(hd + 1) * DN_HEAD_DIM] = o
    for hd in range(DN_HEADS):
        state_ref[hd] = states[hd]


def _delta_rule(q, k, v, gb, masks):
    b, s, _ = q.shape
    tq = TOKENS_DELTA
    tok = lambda width: pl.BlockSpec((1, tq, width), lambda bi, si: (bi, si, 0))
    return pl.pallas_call(
        _delta_kernel,
        grid=(b, s // tq),
        in_specs=[tok(DN_WIDTH), tok(DN_WIDTH), tok(DN_WIDTH), tok(LANES),
                  pl.BlockSpec(masks.shape, lambda bi, si: (0, 0, 0))],
        out_specs=tok(DN_WIDTH),
        out_shape=jax.ShapeDtypeStruct((b, s, DN_WIDTH), F32),
        scratch_shapes=[pltpu.VMEM((DN_HEADS, DN_HEAD_DIM, DN_HEAD_DIM), F32)],
        compiler_params=pltpu.CompilerParams(dimension_semantics=("arbitrary", "arbitrary"),
                                             vmem_limit_bytes=VMEM_LIMIT_BYTES),
        name="delta_rule",
    )(q, k, v, gb, masks)


def _mixer_out_kernel(x_ref, ya_ref, o_ref, z_ref, mod_ref, gdn_ref, ones128_ref, wout_ref, gffn_ref, wgu_ref,
                      wd_ref, gfin_ref, out_ref):
    x = x_ref[0]
    mod = mod_ref[0]
    o = o_ref[0]
    ms = _segment_sum(o * o, ones128_ref[...]) * (1.0 / DN_HEAD_DIM)
    yb = ((o * lax.rsqrt(ms + EPS)) * gdn_ref[...]) * _silu(z_ref[0])
    y = _dot(ya_ref[0], wout_ref[:D_CONV, :]) + _dot(yb.astype(BF16), wout_ref[D_CONV:, :])
    x = x + mod[2:3] * y

    h = _adaln(x, gffn_ref[...], mod[3:4], mod[4:5]).astype(BF16)
    acc = None
    for j in range(FFN_HIDDEN // FFN_COLS):
        gate = _dot(h, wgu_ref[:, j * FFN_COLS:(j + 1) * FFN_COLS])
        up = _dot(h, wgu_ref[:, FFN_HIDDEN + j * FFN_COLS:FFN_HIDDEN + (j + 1) * FFN_COLS])
        part = _dot((_silu(gate) * up).astype(BF16), wd_ref[j * FFN_COLS:(j + 1) * FFN_COLS, :])
        acc = part if acc is None else acc + part
    x = x + mod[5:6] * acc
    out_ref[0] = _adaln(x, gfin_ref[...], mod[6:7], mod[7:8])


def _mixer_out(x, ya, o, z, mod, g_dn, ones128, w_out, g_ffn, w_gate_up, w_down, g_final):
    b, s, d = x.shape
    tm = TOKENS_OUT
    const = lambda *shape: pl.BlockSpec(shape, lambda bi, si: (0,) * len(shape))
    resident = lambda *shape: pl.BlockSpec(shape, lambda bi, si: (0,) * len(shape), pipeline_mode=pl.Buffered(1))
    tok = lambda width: pl.BlockSpec((1, tm, width), lambda bi, si: (bi, si, 0))
    return pl.pallas_call(
        _mixer_out_kernel,
        grid=(b, s // tm),
        in_specs=[
            tok(d), tok(D_CONV), tok(DN_WIDTH), tok(DN_WIDTH),
            pl.BlockSpec((1, SUBLANES, d), lambda bi, si: (bi, 0, 0)),
            const(1, DN_WIDTH),
            resident(DN_WIDTH, DN_WIDTH),
            resident(d, d),
            const(1, d),
            resident(d, 2 * FFN_HIDDEN),
            resident(FFN_HIDDEN, d),
            const(1, d),
        ],
        out_specs=tok(d),
        out_shape=jax.ShapeDtypeStruct((b, s, d), F32),
        compiler_params=pltpu.CompilerParams(dimension_semantics=("arbitrary", "arbitrary"),
                                             vmem_limit_bytes=VMEM_LIMIT_BYTES),
        name="mixer_out",
    )(x, ya, o, z, mod, g_dn, ones128, w_out, g_ffn, w_gate_up, w_down, g_final)


def _block_ones(n, block):
    i = np.arange(n)
    return jnp.asarray((i[:, None] // block) == (i[None, :] // block), dtype=BF16)


def _pad_lanes(row):
    return jnp.pad(row.astype(F32), (0, LANES - row.shape[0])).reshape(1, LANES)


def kernel(x, c, w_ada, b_ada, w_ada_final, b_ada_final, g_norm_mix, g_norm_ffn, g_norm_final, w_in, conv_w_mix,
           conv_w_qkv, a_log, dt_bias, g_conv_out, g_dn_out, w_out, w_gate_up, w_down):
    b, s, d = x.shape
    assert d == D_MODEL and w_ada.shape[0] == 1, "one layer of width D_MODEL"
    assert s % TOKENS_IN == 0 and s % TOKENS_DELTA == 0 and s % TOKENS_OUT == 0

    c_pad = jnp.pad(c, ((0, (-b) % SUBLANES), (0, 0)))
    mod = jnp.concatenate([
        _modulation(c_pad, w_ada[0], b_ada[0].reshape(1, -1)),
        _modulation(c_pad, w_ada_final, b_ada_final.reshape(1, -1)),
    ], axis=1)[:b].reshape(b, SUBLANES, d)

    w_in0 = w_in[0]
    w_main = w_in0[:, :N_MAIN].astype(BF16)
    w_ab = jnp.pad(w_in0[:, N_MAIN:], ((0, 0), (0, LANES - 2 * DN_HEADS))).astype(BF16)
    ones64 = _block_ones(D_CONV, CONV_GROUP_DIM)
    ones128 = _block_ones(DN_WIDTH, DN_HEAD_DIM)

    ya, q, k, v, z, gb = _mixer_in(
        x, mod, g_norm_mix[0].reshape(1, d), w_main, w_ab, conv_w_mix[0], conv_w_qkv[0],
        g_conv_out[0].reshape(1, D_CONV), _pad_lanes(a_log[0]), _pad_lanes(dt_bias[0]), ones64, ones128)

    o = _delta_rule(q, k, v, gb, jnp.asarray(_pair_masks()))

    return _mixer_out(
        x, ya, o, z, mod, jnp.tile(g_dn_out[0], DN_HEADS).reshape(1, DN_WIDTH), ones128,
        w_out[0].astype(BF16), g_norm_ffn[0].reshape(1, d), w_gate_up[0].astype(BF16), w_down[0].astype(BF16),
        g_norm_final.reshape(1, d))
```

```python
import numpy as np
import jax
import jax.numpy as jnp
from jax import lax
from jax.experimental import pallas as pl
from jax.experimental.pallas import tpu as pltpu

F32 = jnp.float32
BF16 = jnp.bfloat16

D_MODEL = 1024
D_CONV = 512
CONV_GROUP_DIM = 64
DN_HEADS = 4
DN_HEAD_DIM = 128
DN_WIDTH = DN_HEADS * DN_HEAD_DIM
CHUNK = 64
FFN_HIDDEN = 2816
EPS = 1e-6

LANES = 128
SUBLANES = 8
N_MAIN = 3 * D_CONV + 4 * DN_WIDTH
PAIR = 2 * CHUNK
TOKENS_IN = 512
TOKENS_PREP = 512
PAIRS_PER_PASS = 2
CHUNKS_SCAN = 4
TOKENS_OUT = 512
FFN_COLS = 256
MOD_COLS = 1024
VMEM_LIMIT_BYTES = 56 * 1024 * 1024


def _dot(a, b):
    return jnp.dot(a, b, preferred_element_type=F32)


def _dot_nt(a, b):
    return lax.dot_general(a, b, (((1,), (1,)), ((), ())), preferred_element_type=F32)


def _split2(x):
    hi = x.astype(BF16)
    lo = (x - hi.astype(F32)).astype(BF16)
    return hi, lo


def _split3(x):
    hi = x.astype(BF16)
    r = x - hi.astype(F32)
    mid = r.astype(BF16)
    lo = (r - mid.astype(F32)).astype(BF16)
    return hi, mid, lo


def _dot3(a2, b2):
    return _dot(a2[0], b2[0]) + (_dot(a2[0], b2[1]) + _dot(a2[1], b2[0]))


def _exact_lhs_dot(lhs_bf16, x):
    h, m, l = _split3(x)
    return _dot(lhs_bf16, h) + (_dot(lhs_bf16, m) + _dot(lhs_bf16, l))


def _segment_sum(x, ones_bf16):
    hi, lo = _split2(x)
    return _dot(hi, ones_bf16) + _dot(lo, ones_bf16)


def _sigmoid(x):
    return 1.0 / (1.0 + jnp.exp(-x))


def _silu(x):
    return x * _sigmoid(x)


def _adaln(x, gain, shift, scale):
    ms = jnp.mean(x * x, axis=-1, keepdims=True)
    return (x * lax.rsqrt(ms + EPS)) * gain * (1.0 + scale) + shift


def _mod_kernel(c_ref, w_ref, b_ref, o_ref):
    c = c_ref[...]
    ca = _split2(_silu(c))
    o_ref[...] = _dot3(ca, _split2(w_ref[...])) + b_ref[...]


def _modulation(c_pad, w, b):
    rows, d = c_pad.shape
    n = w.shape[1]
    return pl.pallas_call(
        _mod_kernel,
        grid=(n // MOD_COLS,),
        in_specs=[
            pl.BlockSpec((rows, d), lambda j: (0, 0)),
            pl.BlockSpec((d, MOD_COLS), lambda j: (0, j)),
            pl.BlockSpec((1, MOD_COLS), lambda j: (0, j)),
        ],
        out_specs=pl.BlockSpec((rows, MOD_COLS), lambda j: (0, j)),
        out_shape=jax.ShapeDtypeStruct((rows, n), F32),
        compiler_params=pltpu.CompilerParams(dimension_semantics=("arbitrary",)),
        name="modulation",
    )(c_pad, w, b)


def _shift_rows(u, tail, j):
    r = pltpu.roll(u, j, 0)
    rt = pltpu.roll(tail, j, 0)
    rows = lax.broadcasted_iota(jnp.int32, tail.shape, 0)
    head = jnp.where(rows < j, rt, r[:SUBLANES])
    return jnp.concatenate([head, r[SUBLANES:]], axis=0)


def _causal_conv(u, tail, w):
    k = w.shape[0]
    out = u * w[k - 1:k]
    for j in range(1, k):
        out = out + _shift_rows(u, tail, j) * w[k - 1 - j:k - j]
    return out


def _mixer_in_kernel(x_ref, mod_ref, gmix_ref, w_ref, wab_ref, cwm_ref, cwq_ref, gco_ref, alog_ref, dtb_ref,
                     ones64_ref, ones128_ref,
                     ya_ref, q_ref, k_ref, v_ref, z_ref, gb_ref,
                     tail_u, tail_qkv):
    @pl.when(pl.program_id(1) == 0)
    def _():
        tail_u[...] = jnp.zeros_like(tail_u)
        tail_qkv[...] = jnp.zeros_like(tail_qkv)

    x = x_ref[0]
    mod = mod_ref[0]
    h = _adaln(x, gmix_ref[...], mod[0:1], mod[1:2]).astype(BF16)

    def proj(i):
        return _dot(h, w_ref[:, i * D_CONV:(i + 1) * D_CONV])

    gate_b = proj(0)
    u = proj(1) * proj(2)
    ya = gate_b * _causal_conv(u, tail_u[...], cwm_ref[...])
    tail_u[...] = u[-SUBLANES:]
    ms = _segment_sum(ya * ya, ones64_ref[...]) * (1.0 / CONV_GROUP_DIM)
    ya_ref[0] = ((ya * lax.rsqrt(ms + EPS)) * gco_ref[...]).astype(ya_ref.dtype)

    tails = tail_qkv[...]
    cwq = cwq_ref[...]
    for i, ref in enumerate((q_ref, k_ref, v_ref)):
        cols = slice(i * DN_WIDTH, (i + 1) * DN_WIDTH)
        pre = proj(3 + i)
        t = _silu(_causal_conv(pre, tails[:, cols], cwq[:, cols]))
        tail_qkv[:, cols] = pre[-SUBLANES:]
        if i < 2:
            ss = _segment_sum(t * t, ones128_ref[...])
            t = t * lax.rsqrt(ss + EPS)
        if i == 0:
            t = t * (DN_HEAD_DIM ** -0.5)
        ref[0] = t
    z_ref[0] = proj(6)

    ab = _dot(h, wab_ref[...])
    sp_in = ab + dtb_ref[...]
    softplus = jnp.maximum(sp_in, 0.0) + jnp.log1p(jnp.exp(-jnp.abs(sp_in)))
    g = -jnp.exp(alog_ref[...]) * softplus
    lane = lax.broadcasted_iota(jnp.int32, ab.shape, 1)
    gb_ref[0] = jnp.where(lane < DN_HEADS, g, _sigmoid(ab))


def _mixer_in(x, mod, g_mix, w_main, w_ab, cw_mix, cw_qkv, g_conv_out, alog_row, dtb_row, ones64, ones128):
    b, s, d = x.shape
    tm = TOKENS_IN
    const = lambda *shape: pl.BlockSpec(shape, lambda bi, si: (0,) * len(shape))
    resident = lambda *shape: pl.BlockSpec(shape, lambda bi, si: (0,) * len(shape), pipeline_mode=pl.Buffered(1))
    tok = lambda width: pl.BlockSpec((1, tm, width), lambda bi, si: (bi, si, 0))
    return pl.pallas_call(
        _mixer_in_kernel,
        grid=(b, s // tm),
        in_specs=[
            tok(d),
            pl.BlockSpec((1, SUBLANES, d), lambda bi, si: (bi, 0, 0)),
            const(1, d),
            resident(d, N_MAIN),
            resident(d, LANES),
            const(3, D_CONV),
            const(4, 3 * DN_WIDTH),
            const(1, D_CONV),
            const(1, LANES),
            const(1, LANES),
            resident(D_CONV, D_CONV),
            resident(DN_WIDTH, DN_WIDTH),
        ],
        out_specs=[tok(D_CONV), tok(DN_WIDTH), tok(DN_WIDTH), tok(DN_WIDTH), tok(DN_WIDTH), tok(LANES)],
        out_shape=[
            jax.ShapeDtypeStruct((b, s, D_CONV), BF16),
            jax.ShapeDtypeStruct((b, s, DN_WIDTH), F32),
            jax.ShapeDtypeStruct((b, s, DN_WIDTH), F32),
            jax.ShapeDtypeStruct((b, s, DN_WIDTH), F32),
            jax.ShapeDtypeStruct((b, s, DN_WIDTH), F32),
            jax.ShapeDtypeStruct((b, s, LANES), F32),
        ],
        scratch_shapes=[pltpu.VMEM((SUBLANES, D_CONV), F32), pltpu.VMEM((SUBLANES, 3 * DN_WIDTH), F32)],
        compiler_params=pltpu.CompilerParams(dimension_semantics=("arbitrary", "arbitrary"),
                                             vmem_limit_bytes=VMEM_LIMIT_BYTES),
        name="mixer_in",
    )(x, mod, g_mix, w_main, w_ab, cw_mix, cw_qkv, g_conv_out, alog_row, dtb_row, ones64, ones128)


N_LEVELS = 6
MASK_CAUSAL = 0
MASK_STRICT = 1
MASK_LEVEL0 = 2


def _pair_masks():
    i = np.arange(PAIR)[:, None]
    j = np.arange(PAIR)[None, :]
    same_chunk = (i // CHUNK) == (j // CHUNK)
    out = [same_chunk & (j <= i), same_chunk & (j < i)]
    for l in range(N_LEVELS):
        s = 1 << l
        out.append((j < i) & ((i // s) != (j // s)) & ((i // (2 * s)) == (j // (2 * s))))
    return np.stack(out).astype(np.float32)


def _delta_prep_pass(base, chunk0, q_ref, k_ref, v_ref, gb_ref, masks_ref,
                     at_ref, bt_ref, dec_ref, qeff_ref, oin_ref):
    insts = [(p, hd) for p in range(PAIRS_PER_PASS) for hd in range(DN_HEADS)]
    causal = masks_ref[MASK_CAUSAL]
    strict = masks_ref[MASK_STRICT]
    causal_b = causal.astype(BF16)
    eye = causal - strict
    lane = lax.broadcasted_iota(jnp.int32, (PAIR, PAIR), 1)
    wide = (PAIR, DN_HEAD_DIM)

    def rows(p):
        return pl.ds(pl.multiple_of(base + p * PAIR, PAIR), PAIR)

    def cols(hd):
        return slice(hd * DN_HEAD_DIM, (hd + 1) * DN_HEAD_DIM)

    q = [q_ref[0, rows(p), cols(hd)] for p, hd in insts]
    k = [k_ref[0, rows(p), cols(hd)] for p, hd in insts]
    v = [v_ref[0, rows(p), cols(hd)] for p, hd in insts]
    g_col = [gb_ref[0, rows(p), hd:hd + 1] for p, hd in insts]
    beta = [jnp.broadcast_to(gb_ref[0, rows(p), DN_HEADS + hd:DN_HEADS + hd + 1], wide) for p, hd in insts]
    n = range(len(insts))

    dg = [_exact_lhs_dot(causal_b, jnp.concatenate([g_col[i] * strict, jnp.broadcast_to(g_col[i], wide)], axis=1))
          for i in n]
    gc = [dg[i][:, PAIR:] for i in n]
    decay = [jnp.exp(dg[i][:, :PAIR]) for i in n]
    kb = [k[i] * beta[i] for i in n]
    kq = [_dot_nt(jnp.concatenate([kb[i], q[i]], axis=0).astype(BF16), k[i].astype(BF16)) for i in n]
    low = [jnp.where(strict > 0, kq[i][:PAIR] * decay[i], 0.0) for i in n]
    attn = [jnp.where(causal > 0, kq[i][PAIR:] * decay[i], 0.0).astype(BF16) for i in n]

    t = [eye - low[i] * masks_ref[MASK_LEVEL0] for i in n]
    for l in range(1, N_LEVELS):
        t16 = [t[i].astype(BF16) for i in n]
        x = [_dot(t16[i], (low[i] * masks_ref[MASK_LEVEL0 + l]).astype(BF16)) for i in n]
        t = [t[i] - _dot(x[i].astype(BF16), t16[i]) for i in n]

    e_gc = [jnp.exp(gc[i]) for i in n]
    uw = [_dot(t[i].astype(BF16), jnp.concatenate([v[i] * beta[i], kb[i] * e_gc[i]], axis=1).astype(BF16))
          for i in n]
    a_uw = [_dot(attn[i], uw[i].astype(BF16)) for i in n]

    g_end = [[gc[i][e * CHUNK + CHUNK - 1:(e + 1) * CHUNK] for e in range(2)] for i in n]
    k_dec = []
    for i in n:
        g_last = jnp.concatenate([jnp.broadcast_to(g_end[i][e], (CHUNK, DN_HEAD_DIM)) for e in range(2)], axis=0)
        k_dec.append((k[i] * jnp.exp(g_last - gc[i])).astype(BF16))

    ab = []
    for i in n:
        u_t = uw[i][:, :DN_HEAD_DIM].T
        w_t = uw[i][:, DN_HEAD_DIM:].T
        parts = []
        for e in range(2):
            in_chunk = (lane >= e * CHUNK) & (lane < (e + 1) * CHUNK)
            parts += [jnp.where(in_chunk, w_t, 0.0), jnp.where(in_chunk, u_t, 0.0)]
        ab.append(_dot(jnp.concatenate(parts, axis=0).astype(BF16), k_dec[i]))

    for i, (p, hd) in enumerate(insts):
        qeff_ref[0, rows(p), cols(hd)] = (q[i] * e_gc[i] - a_uw[i][:, DN_HEAD_DIM:]).astype(qeff_ref.dtype)
        oin_ref[0, rows(p), cols(hd)] = a_uw[i][:, :DN_HEAD_DIM]
        for e in range(2):
            c = chunk0 + 2 * p + e
            at_ref[0, hd, c] = ab[i][(2 * e) * DN_HEAD_DIM:(2 * e + 1) * DN_HEAD_DIM].astype(at_ref.dtype)
            bt_ref[0, hd, c] = ab[i][(2 * e + 1) * DN_HEAD_DIM:(2 * e + 2) * DN_HEAD_DIM]
            dec_ref[0, hd, pl.ds(c, 1), :] = jnp.exp(g_end[i][e])


def _delta_prep_kernel(q_ref, k_ref, v_ref, gb_ref, masks_ref, at_ref, bt_ref, dec_ref, qeff_ref, oin_ref):
    tokens_per_pass = PAIRS_PER_PASS * PAIR

    def body(it, carry):
        _delta_prep_pass(it * tokens_per_pass, it * (tokens_per_pass // CHUNK), q_ref, k_ref, v_ref, gb_ref,
                         masks_ref, at_ref, bt_ref, dec_ref, qeff_ref, oin_ref)
        return carry

    lax.fori_loop(0, TOKENS_PREP // tokens_per_pass, body, 0)


def _delta_prep(q, k, v, gb, masks):
    b, s, _ = q.shape
    tq = TOKENS_PREP
    cpb = tq // CHUNK
    nc = s // CHUNK
    tok = lambda width: pl.BlockSpec((1, tq, width), lambda bi, si: (bi, si, 0))
    mat = pl.BlockSpec((1, DN_HEADS, cpb, DN_HEAD_DIM, DN_HEAD_DIM), lambda bi, si: (bi, 0, si, 0, 0))
    return pl.pallas_call(
        _delta_prep_kernel,
        grid=(b, s // tq),
        in_specs=[tok(DN_WIDTH), tok(DN_WIDTH), tok(DN_WIDTH), tok(LANES),
                  pl.BlockSpec(masks.shape, lambda bi, si: (0, 0, 0))],
        out_specs=[mat, mat,
                   pl.BlockSpec((1, DN_HEADS, cpb, LANES), lambda bi, si: (bi, 0, si, 0)),
                   tok(DN_WIDTH), tok(DN_WIDTH)],
        out_shape=[
            jax.ShapeDtypeStruct((b, DN_HEADS, nc, DN_HEAD_DIM, DN_HEAD_DIM), BF16),
            jax.ShapeDtypeStruct((b, DN_HEADS, nc, DN_HEAD_DIM, DN_HEAD_DIM), F32),
            jax.ShapeDtypeStruct((b, DN_HEADS, nc, LANES), F32),
            jax.ShapeDtypeStruct((b, s, DN_WIDTH), BF16),
            jax.ShapeDtypeStruct((b, s, DN_WIDTH), F32),
        ],
        compiler_params=pltpu.CompilerParams(dimension_semantics=("arbitrary", "arbitrary"),
                                             vmem_limit_bytes=VMEM_LIMIT_BYTES),
        name="delta_prep",
    )(q, k, v, gb, masks)


def _delta_scan_kernel(at_ref, bt_ref, dec_ref, qeff_ref, oin_ref, o_ref, state_ref):
    step = pl.program_id(0)

    @pl.when(step == 0)
    def _():
        state_ref[...] = jnp.zeros_like(state_ref)

    nb = at_ref.shape[0]
    chains = [(bi, hd) for bi in range(nb) for hd in range(DN_HEADS)]
    states = [state_ref[bi, hd] for bi, hd in chains]
    for c in range(CHUNKS_SCAN):
        rows = slice(c * CHUNK, (c + 1) * CHUNK)
        for i, (bi, hd) in enumerate(chains):
            cols = slice(hd * DN_HEAD_DIM, (hd + 1) * DN_HEAD_DIM)
            st16 = states[i].astype(BF16)
            o_ref[bi, rows, cols] = oin_ref[bi, rows, cols] + _dot_nt(qeff_ref[bi, rows, cols], st16)
            dec = dec_ref[bi, hd, pl.ds(step * CHUNKS_SCAN + c, 1), :]
            states[i] = states[i] * dec - _dot(st16, at_ref[bi, hd, c]) + bt_ref[bi, hd, c]
    for i, (bi, hd) in enumerate(chains):
        state_ref[bi, hd] = states[i]


def _delta_scan(at, bt, dec, qeff, oin):
    b, s, _ = qeff.shape
    nc = s // CHUNK
    tq = CHUNKS_SCAN * CHUNK
    mat = pl.BlockSpec((b, DN_HEADS, CHUNKS_SCAN, DN_HEAD_DIM, DN_HEAD_DIM), lambda si: (0, 0, si, 0, 0))
    tok = pl.BlockSpec((b, tq, DN_WIDTH), lambda si: (0, si, 0))
    return pl.pallas_call(
        _delta_scan_kernel,
        grid=(nc // CHUNKS_SCAN,),
        in_specs=[mat, mat, pl.BlockSpec((b, DN_HEADS, nc, LANES), lambda si: (0, 0, 0, 0)), tok, tok],
        out_specs=tok,
        out_shape=jax.ShapeDtypeStruct((b, s, DN_WIDTH), F32),
        scratch_shapes=[pltpu.VMEM((b, DN_HEADS, DN_HEAD_DIM, DN_HEAD_DIM), F32)],
        compiler_params=pltpu.CompilerParams(dimension_semantics=("arbitrary",),
                                             vmem_limit_bytes=VMEM_LIMIT_BYTES),
        name="delta_scan",
    )(at, bt, dec, qeff, oin)


def _mixer_out_kernel(x_ref, ya_ref, o_ref, z_ref, mod_ref, gdn_ref, ones128_ref, wout_ref, gffn_ref, wgu_ref,
                      wd_ref, gfin_ref, out_ref):
    x = x_ref[0]
    mod = mod_ref[0]
    o = o_ref[0]
    ms = _segment_sum(o * o, ones128_ref[...]) * (1.0 / DN_HEAD_DIM)
    yb = ((o * lax.rsqrt(ms + EPS)) * gdn_ref[...]) * _silu(z_ref[0])
    y = _dot(ya_ref[0], wout_ref[:D_CONV, :]) + _dot(yb.astype(BF16), wout_ref[D_CONV:, :])
    x = x + mod[2:3] * y

    h = _adaln(x, gffn_ref[...], mod[3:4], mod[4:5]).astype(BF16)
    acc = None
    for j in range(FFN_HIDDEN // FFN_COLS):
        gate = _dot(h, wgu_ref[:, j * FFN_COLS:(j + 1) * FFN_COLS])
        up = _dot(h, wgu_ref[:, FFN_HIDDEN + j * FFN_COLS:FFN_HIDDEN + (j + 1) * FFN_COLS])
        part = _dot((_silu(gate) * up).astype(BF16), wd_ref[j * FFN_COLS:(j + 1) * FFN_COLS, :])
        acc = part if acc is None else acc + part
    x = x + mod[5:6] * acc
    out_ref[0] = _adaln(x, gfin_ref[...], mod[6:7], mod[7:8])


def _mixer_out(x, ya, o, z, mod, g_dn, ones128, w_out, g_ffn, w_gate_up, w_down, g_final):
    b, s, d = x.shape
    tm = TOKENS_OUT
    const = lambda *shape: pl.BlockSpec(shape, lambda bi, si: (0,) * len(shape))
    resident = lambda *shape: pl.BlockSpec(shape, lambda bi, si: (0,) * len(shape), pipeline_mode=pl.Buffered(1))
    tok = lambda width: pl.BlockSpec((1, tm, width), lambda bi, si: (bi, si, 0))
    return pl.pallas_call(
        _mixer_out_kernel,
        grid=(b, s // tm),
        in_specs=[
            tok(d), tok(D_CONV), tok(DN_WIDTH), tok(DN_WIDTH),
            pl.BlockSpec((1, SUBLANES, d), lambda bi, si: (bi, 0, 0)),
            const(1, DN_WIDTH),
            resident(DN_WIDTH, DN_WIDTH),
            resident(d, d),
            const(1, d),
            resident(d, 2 * FFN_HIDDEN),
            resident(FFN_HIDDEN, d),
            const(1, d),
        ],
        out_specs=tok(d),
        out_shape=jax.ShapeDtypeStruct((b, s, d), F32),
        compiler_params=pltpu.CompilerParams(dimension_semantics=("arbitrary", "arbitrary"),
                                             vmem_limit_bytes=VMEM_LIMIT_BYTES),
        name="mixer_out",
    )(x, ya, o, z, mod, g_dn, ones128, w_out, g_ffn, w_gate_up, w_down, g_final)


def _block_ones(n, block):
    i = np.arange(n)
    return jnp.asarray((i[:, None] // block) == (i[None, :] // block), dtype=BF16)


def _pad_lanes(row):
    return jnp.pad(row.astype(F32), (0, LANES - row.shape[0])).reshape(1, LANES)


def kernel(x, c, w_ada, b_ada, w_ada_final, b_ada_final, g_norm_mix, g_norm_ffn, g_norm_final, w_in, conv_w_mix,
           conv_w_qkv, a_log, dt_bias, g_conv_out, g_dn_out, w_out, w_gate_up, w_down):
    b, s, d = x.shape
    assert d == D_MODEL and w_ada.shape[0] == 1, "one layer of width D_MODEL"
    assert s % TOKENS_IN == 0 and s % TOKENS_PREP == 0 and s % TOKENS_OUT == 0
    assert TOKENS_PREP == SUBLANES * CHUNK and (s // CHUNK) % CHUNKS_SCAN == 0

    c_pad = jnp.pad(c, ((0, (-b) % SUBLANES), (0, 0)))
    mod = jnp.concatenate([
        _modulation(c_pad, w_ada[0], b_ada[0].reshape(1, -1)),
        _modulation(c_pad, w_ada_final, b_ada_final.reshape(1, -1)),
    ], axis=1)[:b].reshape(b, SUBLANES, d)

    w_in0 = w_in[0]
    w_main = w_in0[:, :N_MAIN].astype(BF16)
    w_ab = jnp.pad(w_in0[:, N_MAIN:], ((0, 0), (0, LANES - 2 * DN_HEADS))).astype(BF16)
    ones64 = _block_ones(D_CONV, CONV_GROUP_DIM)
    ones128 = _block_ones(DN_WIDTH, DN_HEAD_DIM)

    ya, q, k, v, z, gb = _mixer_in(
        x, mod, g_norm_mix[0].reshape(1, d), w_main, w_ab, conv_w_mix[0], conv_w_qkv[0],
        g_conv_out[0].reshape(1, D_CONV), _pad_lanes(a_log[0]), _pad_lanes(dt_bias[0]), ones64, ones128)

    o = _delta_scan(*_delta_prep(q, k, v, gb, jnp.asarray(_pair_masks())))

    return _mixer_out(
        x, ya, o, z, mod, jnp.tile(g_dn_out[0], DN_HEADS).reshape(1, DN_WIDTH), ones128,
        w_out[0].astype(BF16), g_norm_ffn[0].reshape(1, d), w_gate_up[0].astype(BF16), w_down[0].astype(BF16),
        g_norm_final.reshape(1, d))
```

```python
import numpy as np
import jax
import jax.numpy as jnp
from jax import lax
from jax.experimental import pallas as pl
from jax.experimental.pallas import tpu as pltpu

F32 = jnp.float32
BF16 = jnp.bfloat16

D_MODEL = 1024
D_CONV = 512
CONV_GROUP_DIM = 64
DN_HEADS = 4
DN_HEAD_DIM = 128
DN_WIDTH = DN_HEADS * DN_HEAD_DIM
CHUNK = 64
FFN_HIDDEN = 2816
EPS = 1e-6

LANES = 128
SUBLANES = 8
MXU_COLS = 256
N_MAIN = 3 * D_CONV + 4 * DN_WIDTH
PAIR = 2 * CHUNK
TOKENS_IN = 512
TOKENS_PREP = 512
PAIRS_PER_PASS = 2
CHUNKS_SCAN = 4
TOKENS_OUT = 512
FFN_COLS = 256
MOD_COLS = 1024
VMEM_LIMIT_BYTES = 56 * 1024 * 1024


def _dot(a, b):
    return jnp.dot(a, b, preferred_element_type=F32)


def _dot_nt(a, b):
    return lax.dot_general(a, b, (((1,), (1,)), ((), ())), preferred_element_type=F32)


def _split2(x):
    hi = x.astype(BF16)
    lo = (x - hi.astype(F32)).astype(BF16)
    return hi, lo


def _dot3(a2, b2):
    return _dot(a2[0], b2[0]) + (_dot(a2[0], b2[1]) + _dot(a2[1], b2[0]))


def _segment_sum(x, ones_bf16):
    xb = x.astype(BF16)
    w = ones_bf16.shape[0]
    return jnp.concatenate([_dot(xb[:, j:j + w], ones_bf16) for j in range(0, x.shape[1], w)], axis=1)


def _sigmoid(x):
    return 1.0 / (1.0 + jnp.exp(-x))


def _silu(x):
    return x * _sigmoid(x)


def _adaln(x, gain, shift, scale):
    ms = jnp.mean(x * x, axis=-1, keepdims=True)
    return (x * lax.rsqrt(ms + EPS)) * gain * (1.0 + scale) + shift


def _mod_kernel(c_ref, w_ref, b_ref, o_ref):
    c = c_ref[...]
    ca = _split2(_silu(c))
    o_ref[...] = _dot3(ca, _split2(w_ref[...])) + b_ref[...]


def _modulation(c_pad, w, b):
    rows, d = c_pad.shape
    n = w.shape[1]
    return pl.pallas_call(
        _mod_kernel,
        grid=(n // MOD_COLS,),
        in_specs=[
            pl.BlockSpec((rows, d), lambda j: (0, 0)),
            pl.BlockSpec((d, MOD_COLS), lambda j: (0, j)),
            pl.BlockSpec((1, MOD_COLS), lambda j: (0, j)),
        ],
        out_specs=pl.BlockSpec((rows, MOD_COLS), lambda j: (0, j)),
        out_shape=jax.ShapeDtypeStruct((rows, n), F32),
        compiler_params=pltpu.CompilerParams(dimension_semantics=("arbitrary",)),
        name="modulation",
    )(c_pad, w, b)


def _shift_rows(u, tail, j):
    r = pltpu.roll(u, j, 0)
    rt = pltpu.roll(tail, j, 0)
    rows = lax.broadcasted_iota(jnp.int32, tail.shape, 0)
    head = jnp.where(rows < j, rt, r[:SUBLANES])
    return jnp.concatenate([head, r[SUBLANES:]], axis=0)


def _causal_conv(u, tail, w):
    k = w.shape[0]
    out = u * w[k - 1:k]
    for j in range(1, k):
        out = out + _shift_rows(u, tail, j) * w[k - 1 - j:k - j]
    return out


def _mixer_in_kernel(x_ref, mod_ref, gmix_ref, w_ref, wab_ref, cwm_ref, cwq_ref, gco_ref, alog_ref, dtb_ref,
                     ones64_ref, ones128_ref,
                     ya_ref, q_ref, k_ref, v_ref, z_ref, gb_ref,
                     tail_u, tail_qkv):
    @pl.when(pl.program_id(1) == 0)
    def _():
        tail_u[...] = jnp.zeros_like(tail_u)
        tail_qkv[...] = jnp.zeros_like(tail_qkv)

    x = x_ref[0]
    mod = mod_ref[0]
    h = _adaln(x, gmix_ref[...], mod[0:1], mod[1:2]).astype(BF16)

    def proj(i):
        return _dot(h, w_ref[:, i * D_CONV:(i + 1) * D_CONV])

    pre_qkv = [proj(3 + i) for i in range(3)]
    gate_c, h_conv, gate_b = proj(1), proj(2), proj(0)
    z_ref[0] = proj(6)
    ab = _dot(h, wab_ref[...])

    tails = tail_qkv[...]
    cwq = cwq_ref[...]
    act = []
    for i in range(3):
        cols = slice(i * DN_WIDTH, (i + 1) * DN_WIDTH)
        act.append(_silu(_causal_conv(pre_qkv[i], tails[:, cols], cwq[:, cols])))
        tail_qkv[:, cols] = pre_qkv[i][-SUBLANES:]
    v_ref[0] = act[2]

    u = gate_c * h_conv
    ya = gate_b * _causal_conv(u, tail_u[...], cwm_ref[...])
    tail_u[...] = u[-SUBLANES:]

    for i, ref in enumerate((q_ref, k_ref)):
        ss = _segment_sum(act[i] * act[i], ones128_ref[...])
        scale = DN_HEAD_DIM ** -0.5 if i == 0 else 1.0
        ref[0] = act[i] * lax.rsqrt(ss + EPS) * scale
    ms = _segment_sum(ya * ya, ones64_ref[...]) * (1.0 / CONV_GROUP_DIM)
    ya_ref[0] = ((ya * lax.rsqrt(ms + EPS)) * gco_ref[...]).astype(ya_ref.dtype)

    sp_in = ab + dtb_ref[...]
    softplus = jnp.maximum(sp_in, 0.0) + jnp.log1p(jnp.exp(-jnp.abs(sp_in)))
    g = -jnp.exp(alog_ref[...]) * softplus
    lane = lax.broadcasted_iota(jnp.int32, ab.shape, 1)
    gb_ref[0] = jnp.where(lane < DN_HEADS, g, _sigmoid(ab))


def _mixer_in(x, mod, g_mix, w_main, w_ab, cw_mix, cw_qkv, g_conv_out, alog_row, dtb_row, ones64, ones128):
    b, s, d = x.shape
    tm = TOKENS_IN
    const = lambda *shape: pl.BlockSpec(shape, lambda bi, si: (0,) * len(shape))
    resident = lambda *shape: pl.BlockSpec(shape, lambda bi, si: (0,) * len(shape), pipeline_mode=pl.Buffered(1))
    tok = lambda width: pl.BlockSpec((1, tm, width), lambda bi, si: (bi, si, 0))
    return pl.pallas_call(
        _mixer_in_kernel,
        grid=(b, s // tm),
        in_specs=[
            tok(d),
            pl.BlockSpec((1, SUBLANES, d), lambda bi, si: (bi, 0, 0)),
            const(1, d),
            resident(*w_main.shape),
            resident(d, LANES),
            const(3, D_CONV),
            const(4, 3 * DN_WIDTH),
            const(1, D_CONV),
            const(1, LANES),
            const(1, LANES),
            const(MXU_COLS, MXU_COLS),
            const(MXU_COLS, MXU_COLS),
        ],
        out_specs=[tok(D_CONV), tok(DN_WIDTH), tok(DN_WIDTH), tok(DN_WIDTH), tok(DN_WIDTH), tok(LANES)],
        out_shape=[
            jax.ShapeDtypeStruct((b, s, D_CONV), BF16),
            jax.ShapeDtypeStruct((b, s, DN_WIDTH), F32),
            jax.ShapeDtypeStruct((b, s, DN_WIDTH), F32),
            jax.ShapeDtypeStruct((b, s, DN_WIDTH), F32),
            jax.ShapeDtypeStruct((b, s, DN_WIDTH), F32),
            jax.ShapeDtypeStruct((b, s, LANES), F32),
        ],
        scratch_shapes=[pltpu.VMEM((SUBLANES, D_CONV), F32), pltpu.VMEM((SUBLANES, 3 * DN_WIDTH), F32)],
        compiler_params=pltpu.CompilerParams(dimension_semantics=("arbitrary", "arbitrary"),
                                             vmem_limit_bytes=VMEM_LIMIT_BYTES),
        name="mixer_in",
    )(x, mod, g_mix, w_main, w_ab, cw_mix, cw_qkv, g_conv_out, alog_row, dtb_row, ones64, ones128)


N_LEVELS = 6
MASK_CAUSAL = 0
MASK_STRICT = 1
MASK_LEVEL0 = 2


def _pair_masks():
    i = np.arange(PAIR)[:, None]
    j = np.arange(PAIR)[None, :]
    same_chunk = (i // CHUNK) == (j // CHUNK)
    out = [same_chunk & (j <= i), same_chunk & (j < i)]
    for l in range(N_LEVELS):
        s = 1 << l
        out.append((j < i) & ((i // s) != (j // s)) & ((i // (2 * s)) == (j // (2 * s))))
    return np.stack(out).astype(np.float32)


def _delta_prep_pass(base, chunk0, q_ref, k_ref, v_ref, gb_ref, masks_ref,
                     at_ref, bt_ref, dec_ref, qeff_ref, oin_ref):
    insts = [(p, hd) for p in range(PAIRS_PER_PASS) for hd in range(DN_HEADS)]
    causal = masks_ref[MASK_CAUSAL]
    strict = masks_ref[MASK_STRICT]
    eye = causal - strict
    lane = lax.broadcasted_iota(jnp.int32, (PAIR, PAIR), 1)
    wide = (PAIR, DN_HEAD_DIM)

    def rows(p):
        return pl.ds(pl.multiple_of(base + p * PAIR, PAIR), PAIR)

    def cols(hd):
        return slice(hd * DN_HEAD_DIM, (hd + 1) * DN_HEAD_DIM)

    q = [q_ref[0, rows(p), cols(hd)] for p, hd in insts]
    k = [k_ref[0, rows(p), cols(hd)] for p, hd in insts]
    v = [v_ref[0, rows(p), cols(hd)] for p, hd in insts]
    g_col = [gb_ref[0, rows(p), hd:hd + 1] for p, hd in insts]
    beta = [jnp.broadcast_to(gb_ref[0, rows(p), DN_HEADS + hd:DN_HEADS + hd + 1], wide) for p, hd in insts]
    n = range(len(insts))

    row_in_chunk = lax.broadcasted_iota(jnp.int32, wide, 0) & (CHUNK - 1)
    gc = [jnp.broadcast_to(g_col[i], wide) for i in n]
    for l in range(N_LEVELS):
        s = 1 << l
        gc = [gc[i] + jnp.where(row_in_chunk >= s, pltpu.roll(gc[i], s, 0), 0.0) for i in n]
    decay = [jnp.exp(gc[i] - gc[i].T) for i in n]
    kb = [k[i] * beta[i] for i in n]
    kq = [_dot_nt(jnp.concatenate([kb[i], q[i]], axis=0).astype(BF16), k[i].astype(BF16)) for i in n]
    low = [jnp.where(strict > 0, kq[i][:PAIR] * decay[i], 0.0) for i in n]
    attn = [jnp.where(causal > 0, kq[i][PAIR:] * decay[i], 0.0).astype(BF16) for i in n]

    t = [eye - low[i] * masks_ref[MASK_LEVEL0] for i in n]
    for l in range(1, N_LEVELS):
        t16 = [t[i].astype(BF16) for i in n]
        x = [_dot(t16[i], (low[i] * masks_ref[MASK_LEVEL0 + l]).astype(BF16)) for i in n]
        t = [t[i] - _dot(x[i].astype(BF16), t16[i]) for i in n]

    e_gc = [jnp.exp(gc[i]) for i in n]
    uw = [_dot(t[i].astype(BF16), jnp.concatenate([v[i] * beta[i], kb[i] * e_gc[i]], axis=1).astype(BF16))
          for i in n]
    a_uw = [_dot(attn[i], uw[i].astype(BF16)) for i in n]

    g_end = [[gc[i][e * CHUNK + CHUNK - 1:(e + 1) * CHUNK] for e in range(2)] for i in n]
    k_dec = []
    for i in n:
        g_last = jnp.concatenate([jnp.broadcast_to(g_end[i][e], (CHUNK, DN_HEAD_DIM)) for e in range(2)], axis=0)
        k_dec.append((k[i] * jnp.exp(g_last - gc[i])).astype(BF16))

    ab = []
    for i in n:
        u_t = uw[i][:, :DN_HEAD_DIM].T
        w_t = uw[i][:, DN_HEAD_DIM:].T
        parts = []
        for e in range(2):
            in_chunk = (lane >= e * CHUNK) & (lane < (e + 1) * CHUNK)
            parts += [jnp.where(in_chunk, w_t, 0.0), jnp.where(in_chunk, u_t, 0.0)]
        ab.append(_dot(jnp.concatenate(parts, axis=0).astype(BF16), k_dec[i]))

    for i, (p, hd) in enumerate(insts):
        qeff_ref[0, rows(p), cols(hd)] = (q[i] * e_gc[i] - a_uw[i][:, DN_HEAD_DIM:]).astype(qeff_ref.dtype)
        oin_ref[0, rows(p), cols(hd)] = a_uw[i][:, :DN_HEAD_DIM]
        for e in range(2):
            c = chunk0 + 2 * p + e
            at_ref[0, hd, c] = ab[i][(2 * e) * DN_HEAD_DIM:(2 * e + 1) * DN_HEAD_DIM].astype(at_ref.dtype)
            bt_ref[0, hd, c] = ab[i][(2 * e + 1) * DN_HEAD_DIM:(2 * e + 2) * DN_HEAD_DIM]
            dec_ref[0, hd, pl.ds(c, 1), :] = jnp.exp(g_end[i][e])


def _delta_prep_kernel(q_ref, k_ref, v_ref, gb_ref, masks_ref, at_ref, bt_ref, dec_ref, qeff_ref, oin_ref):
    tokens_per_pass = PAIRS_PER_PASS * PAIR

    def body(it, carry):
        _delta_prep_pass(it * tokens_per_pass, it * (tokens_per_pass // CHUNK), q_ref, k_ref, v_ref, gb_ref,
                         masks_ref, at_ref, bt_ref, dec_ref, qeff_ref, oin_ref)
        return carry

    lax.fori_loop(0, TOKENS_PREP // tokens_per_pass, body, 0)


def _delta_prep(q, k, v, gb, masks):
    b, s, _ = q.shape
    tq = TOKENS_PREP
    cpb = tq // CHUNK
    nc = s // CHUNK
    tok = lambda width: pl.BlockSpec((1, tq, width), lambda bi, si: (bi, si, 0))
    mat = pl.BlockSpec((1, DN_HEADS, cpb, DN_HEAD_DIM, DN_HEAD_DIM), lambda bi, si: (bi, 0, si, 0, 0))
    return pl.pallas_call(
        _delta_prep_kernel,
        grid=(b, s // tq),
        in_specs=[tok(DN_WIDTH), tok(DN_WIDTH), tok(DN_WIDTH), tok(LANES),
                  pl.BlockSpec(masks.shape, lambda bi, si: (0, 0, 0))],
        out_specs=[mat, mat,
                   pl.BlockSpec((1, DN_HEADS, cpb, LANES), lambda bi, si: (bi, 0, si, 0)),
                   tok(DN_WIDTH), tok(DN_WIDTH)],
        out_shape=[
            jax.ShapeDtypeStruct((b, DN_HEADS, nc, DN_HEAD_DIM, DN_HEAD_DIM), BF16),
            jax.ShapeDtypeStruct((b, DN_HEADS, nc, DN_HEAD_DIM, DN_HEAD_DIM), F32),
            jax.ShapeDtypeStruct((b, DN_HEADS, nc, LANES), F32),
            jax.ShapeDtypeStruct((b, s, DN_WIDTH), BF16),
            jax.ShapeDtypeStruct((b, s, DN_WIDTH), F32),
        ],
        compiler_params=pltpu.CompilerParams(dimension_semantics=("arbitrary", "arbitrary"),
                                             vmem_limit_bytes=VMEM_LIMIT_BYTES),
        name="delta_prep",
    )(q, k, v, gb, masks)


def _delta_scan_kernel(at_ref, bt_ref, dec_ref, qeff_ref, oin_ref, o_ref, state_ref):
    step = pl.program_id(0)

    @pl.when(step == 0)
    def _():
        state_ref[...] = jnp.zeros_like(state_ref)

    nb = at_ref.shape[0]
    chains = [(bi, hd) for bi in range(nb) for hd in range(DN_HEADS)]
    states = [state_ref[bi, hd] for bi, hd in chains]
    for c in range(CHUNKS_SCAN):
        rows = slice(c * CHUNK, (c + 1) * CHUNK)
        for i, (bi, hd) in enumerate(chains):
            cols = slice(hd * DN_HEAD_DIM, (hd + 1) * DN_HEAD_DIM)
            st16 = states[i].astype(BF16)
            o_ref[bi, rows, cols] = oin_ref[bi, rows, cols] + _dot_nt(qeff_ref[bi, rows, cols], st16)
            dec = dec_ref[bi, hd, pl.ds(step * CHUNKS_SCAN + c, 1), :]
            states[i] = states[i] * dec - _dot(st16, at_ref[bi, hd, c]) + bt_ref[bi, hd, c]
    for i, (bi, hd) in enumerate(chains):
        state_ref[bi, hd] = states[i]


def _delta_scan(at, bt, dec, qeff, oin):
    b, s, _ = qeff.shape
    nc = s // CHUNK
    tq = CHUNKS_SCAN * CHUNK
    mat = pl.BlockSpec((b, DN_HEADS, CHUNKS_SCAN, DN_HEAD_DIM, DN_HEAD_DIM), lambda si: (0, 0, si, 0, 0))
    tok = pl.BlockSpec((b, tq, DN_WIDTH), lambda si: (0, si, 0))
    return pl.pallas_call(
        _delta_scan_kernel,
        grid=(nc // CHUNKS_SCAN,),
        in_specs=[mat, mat, pl.BlockSpec((b, DN_HEADS, nc, LANES), lambda si: (0, 0, 0, 0)), tok, tok],
        out_specs=tok,
        out_shape=jax.ShapeDtypeStruct((b, s, DN_WIDTH), F32),
        scratch_shapes=[pltpu.VMEM((b, DN_HEADS, DN_HEAD_DIM, DN_HEAD_DIM), F32)],
        compiler_params=pltpu.CompilerParams(dimension_semantics=("arbitrary",),
                                             vmem_limit_bytes=VMEM_LIMIT_BYTES),
        name="delta_scan",
    )(at, bt, dec, qeff, oin)


def _mixer_out_kernel(x_ref, ya_ref, o_ref, z_ref, mod_ref, gdn_ref, ones128_ref, wout_ref, gffn_ref, wgu_ref,
                      wd_ref, gfin_ref, out_ref):
    x = x_ref[0]
    mod = mod_ref[0]
    o = o_ref[0]
    ms = _segment_sum(o * o, ones128_ref[...]) * (1.0 / DN_HEAD_DIM)
    yb = ((o * lax.rsqrt(ms + EPS)) * gdn_ref[...]) * _silu(z_ref[0])
    y = _dot(ya_ref[0], wout_ref[:D_CONV, :]) + _dot(yb.astype(BF16), wout_ref[D_CONV:, :])
    x = x + mod[2:3] * y

    h = _adaln(x, gffn_ref[...], mod[3:4], mod[4:5]).astype(BF16)
    acc = None
    for j in range(FFN_HIDDEN // FFN_COLS):
        gate = _dot(h, wgu_ref[:, j * FFN_COLS:(j + 1) * FFN_COLS])
        up = _dot(h, wgu_ref[:, FFN_HIDDEN + j * FFN_COLS:FFN_HIDDEN + (j + 1) * FFN_COLS])
        part = _dot((_silu(gate) * up).astype(BF16), wd_ref[j * FFN_COLS:(j + 1) * FFN_COLS, :])
        acc = part if acc is None else acc + part
    x = x + mod[5:6] * acc
    out_ref[0] = _adaln(x, gfin_ref[...], mod[6:7], mod[7:8])


def _mixer_out(x, ya, o, z, mod, g_dn, ones128, w_out, g_ffn, w_gate_up, w_down, g_final):
    b, s, d = x.shape
    tm = TOKENS_OUT
    const = lambda *shape: pl.BlockSpec(shape, lambda bi, si: (0,) * len(shape))
    resident = lambda *shape: pl.BlockSpec(shape, lambda bi, si: (0,) * len(shape), pipeline_mode=pl.Buffered(1))
    tok = lambda width: pl.BlockSpec((1, tm, width), lambda bi, si: (bi, si, 0))
    return pl.pallas_call(
        _mixer_out_kernel,
        grid=(b, s // tm),
        in_specs=[
            tok(d), tok(D_CONV), tok(DN_WIDTH), tok(DN_WIDTH),
            pl.BlockSpec((1, SUBLANES, d), lambda bi, si: (bi, 0, 0)),
            const(1, DN_WIDTH),
            const(MXU_COLS, MXU_COLS),
            resident(d, d),
            const(1, d),
            resident(d, 2 * FFN_HIDDEN),
            resident(FFN_HIDDEN, d),
            const(1, d),
        ],
        out_specs=tok(d),
        out_shape=jax.ShapeDtypeStruct((b, s, d), F32),
        compiler_params=pltpu.CompilerParams(dimension_semantics=("arbitrary", "arbitrary"),
                                             vmem_limit_bytes=VMEM_LIMIT_BYTES),
        name="mixer_out",
    )(x, ya, o, z, mod, g_dn, ones128, w_out, g_ffn, w_gate_up, w_down, g_final)


def _block_ones(n, block):
    i = np.arange(n)
    return jnp.asarray((i[:, None] // block) == (i[None, :] // block), dtype=BF16)


def _pad_lanes(row):
    return jnp.pad(row.astype(F32), (0, LANES - row.shape[0])).reshape(1, LANES)


def kernel(x, c, w_ada, b_ada, w_ada_final, b_ada_final, g_norm_mix, g_norm_ffn, g_norm_final, w_in, conv_w_mix,
           conv_w_qkv, a_log, dt_bias, g_conv_out, g_dn_out, w_out, w_gate_up, w_down):
    b, s, d = x.shape
    assert d == D_MODEL and w_ada.shape[0] == 1, "one layer of width D_MODEL"
    assert s % TOKENS_IN == 0 and s % TOKENS_PREP == 0 and s % TOKENS_OUT == 0
    assert TOKENS_PREP == SUBLANES * CHUNK and (s // CHUNK) % CHUNKS_SCAN == 0

    c_pad = jnp.pad(c, ((0, (-b) % SUBLANES), (0, 0)))
    mod = jnp.concatenate([
        _modulation(c_pad, w_ada[0], b_ada[0].reshape(1, -1)),
        _modulation(c_pad, w_ada_final, b_ada_final.reshape(1, -1)),
    ], axis=1)[:b].reshape(b, SUBLANES, d)

    w_main = w_in[0].astype(BF16)
    w_ab = jnp.pad(w_main[:, N_MAIN:], ((0, 0), (0, LANES - 2 * DN_HEADS)))
    ones64 = _block_ones(MXU_COLS, CONV_GROUP_DIM)
    ones128 = _block_ones(MXU_COLS, DN_HEAD_DIM)

    ya, q, k, v, z, gb = _mixer_in(
        x, mod, g_norm_mix[0].reshape(1, d), w_main, w_ab, conv_w_mix[0], conv_w_qkv[0],
        g_conv_out[0].reshape(1, D_CONV), _pad_lanes(a_log[0]), _pad_lanes(dt_bias[0]), ones64, ones128)

    o = _delta_scan(*_delta_prep(q, k, v, gb, jnp.asarray(_pair_masks())))

    return _mixer_out(
        x, ya, o, z, mod, jnp.tile(g_dn_out[0], DN_HEADS).reshape(1, DN_WIDTH), ones128,
        w_out[0].astype(BF16), g_norm_ffn[0].reshape(1, d), w_gate_up[0].astype(BF16), w_down[0].astype(BF16),
        g_norm_final.reshape(1, d))
```

```python
import numpy as np
import jax
import jax.numpy as jnp
from jax import lax
from jax.experimental import pallas as pl
from jax.experimental.pallas import tpu as pltpu

F32 = jnp.float32
BF16 = jnp.bfloat16

D_MODEL = 1024
D_CONV = 512
CONV_GROUP_DIM = 64
DN_HEADS = 4
DN_HEAD_DIM = 128
DN_WIDTH = DN_HEADS * DN_HEAD_DIM
CHUNK = 64
FFN_HIDDEN = 2816
EPS = 1e-6

LANES = 128
SUBLANES = 8
MXU_COLS = 256
N_MAIN = 3 * D_CONV + 4 * DN_WIDTH
PAIR = 2 * CHUNK
N_LEVELS = 6
GB_GC = 0 * DN_HEADS
GB_BETA = 1 * DN_HEADS
GB_EXP_GC = 2 * DN_HEADS
GB_EXP_REST = 3 * DN_HEADS
GB_EXP_LAST = 4 * DN_HEADS
TOKENS_IN = 512
TOKENS_PREP = 1024
PAIRS_PER_PASS = 4
CHUNKS_SCAN = 4
TOKENS_OUT = 512
FFN_COLS = 256
MOD_COLS = 1024
VMEM_LIMIT_BYTES = 56 * 1024 * 1024


def _dot(a, b):
    return jnp.dot(a, b, preferred_element_type=F32)


def _dot_nt(a, b):
    return lax.dot_general(a, b, (((1,), (1,)), ((), ())), preferred_element_type=F32)


def _split2(x):
    hi = x.astype(BF16)
    lo = (x - hi.astype(F32)).astype(BF16)
    return hi, lo


def _dot3(a2, b2):
    return _dot(a2[0], b2[0]) + (_dot(a2[0], b2[1]) + _dot(a2[1], b2[0]))


def _segment_sum(x, ones_bf16):
    xb = x.astype(BF16)
    w = ones_bf16.shape[0]
    return jnp.concatenate([_dot(xb[:, j:j + w], ones_bf16) for j in range(0, x.shape[1], w)], axis=1)


def _sigmoid(x):
    return 1.0 / (1.0 + jnp.exp(-x))


def _silu(x):
    return x * _sigmoid(x)


def _adaln(x, gain, shift, scale):
    ms = jnp.mean(x * x, axis=-1, keepdims=True)
    return (x * lax.rsqrt(ms + EPS)) * gain * (1.0 + scale) + shift


def _mod_kernel(c_ref, w_ref, b_ref, o_ref):
    c = c_ref[...]
    ca = _split2(_silu(c))
    o_ref[...] = _dot3(ca, _split2(w_ref[...])) + b_ref[...]


def _modulation(c_pad, w, b):
    rows, d = c_pad.shape
    n = w.shape[1]
    return pl.pallas_call(
        _mod_kernel,
        grid=(n // MOD_COLS,),
        in_specs=[
            pl.BlockSpec((rows, d), lambda j: (0, 0)),
            pl.BlockSpec((d, MOD_COLS), lambda j: (0, j)),
            pl.BlockSpec((1, MOD_COLS), lambda j: (0, j)),
        ],
        out_specs=pl.BlockSpec((rows, MOD_COLS), lambda j: (0, j)),
        out_shape=jax.ShapeDtypeStruct((rows, n), F32),
        compiler_params=pltpu.CompilerParams(dimension_semantics=("arbitrary",)),
        name="modulation",
    )(c_pad, w, b)


def _shift_rows(u, tail, j):
    r = pltpu.roll(u, j, 0)
    rt = pltpu.roll(tail, j, 0)
    rows = lax.broadcasted_iota(jnp.int32, tail.shape, 0)
    head = jnp.where(rows < j, rt, r[:SUBLANES])
    return jnp.concatenate([head, r[SUBLANES:]], axis=0)


def _causal_conv(u, tail, w):
    k = w.shape[0]
    out = u * w[k - 1:k]
    for j in range(1, k):
        out = out + _shift_rows(u, tail, j) * w[k - 1 - j:k - j]
    return out


def _mixer_in_kernel(x_ref, mod_ref, gmix_ref, w_ref, wab_ref, cwm_ref, cwq_ref, gco_ref, alog_ref, dtb_ref,
                     ones64_ref, ones128_ref,
                     ya_ref, q_ref, k_ref, v_ref, z_ref, gb_ref,
                     tail_u, tail_qkv):
    @pl.when(pl.program_id(1) == 0)
    def _():
        tail_u[...] = jnp.zeros_like(tail_u)
        tail_qkv[...] = jnp.zeros_like(tail_qkv)

    x = x_ref[0]
    mod = mod_ref[0]
    h = _adaln(x, gmix_ref[...], mod[0:1], mod[1:2]).astype(BF16)

    def proj(i):
        return _dot(h, w_ref[:, i * D_CONV:(i + 1) * D_CONV])

    pre_qkv = [proj(3 + i) for i in range(3)]
    gate_c, h_conv, gate_b = proj(1), proj(2), proj(0)
    z_ref[0] = proj(6)
    ab = _dot(h, wab_ref[...])

    tails = tail_qkv[...]
    cwq = cwq_ref[...]
    act = []
    for i in range(3):
        cols = slice(i * DN_WIDTH, (i + 1) * DN_WIDTH)
        act.append(_silu(_causal_conv(pre_qkv[i], tails[:, cols], cwq[:, cols])))
        tail_qkv[:, cols] = pre_qkv[i][-SUBLANES:]
    v_ref[0] = act[2]

    u = gate_c * h_conv
    ya = gate_b * _causal_conv(u, tail_u[...], cwm_ref[...])
    tail_u[...] = u[-SUBLANES:]

    for i, ref in enumerate((q_ref, k_ref)):
        ss = _segment_sum(act[i] * act[i], ones128_ref[...])
        scale = DN_HEAD_DIM ** -0.5 if i == 0 else 1.0
        ref[0] = act[i] * lax.rsqrt(ss + EPS) * scale
    ms = _segment_sum(ya * ya, ones64_ref[...]) * (1.0 / CONV_GROUP_DIM)
    ya_ref[0] = ((ya * lax.rsqrt(ms + EPS)) * gco_ref[...]).astype(ya_ref.dtype)

    sp_in = ab + dtb_ref[...]
    softplus = jnp.maximum(sp_in, 0.0) + jnp.log1p(jnp.exp(-jnp.abs(sp_in)))
    g = -jnp.exp(alog_ref[...]) * softplus
    row_in_chunk = lax.broadcasted_iota(jnp.int32, ab.shape, 0) & (CHUNK - 1)
    gc = g
    for l in range(N_LEVELS):
        s = 1 << l
        gc = gc + jnp.where(row_in_chunk >= s, pltpu.roll(gc, s, 0), 0.0)
    g_last = jnp.concatenate(
        [jnp.broadcast_to(gc[c * CHUNK + CHUNK - 1:(c + 1) * CHUNK], (CHUNK, LANES)) for c in range(ab.shape[0] // CHUNK)],
        axis=0)
    lane = lax.broadcasted_iota(jnp.int32, ab.shape, 1)
    gb_ref[0] = jnp.where(
        lane < GB_BETA, gc, jnp.where(
            lane < GB_EXP_GC, _sigmoid(ab), jnp.where(
                lane < GB_EXP_REST, jnp.exp(gc), jnp.where(
                    lane < GB_EXP_LAST, jnp.exp(g_last - gc), jnp.exp(g_last)))))


def _mixer_in(x, mod, g_mix, w_main, w_ab, cw_mix, cw_qkv, g_conv_out, alog_row, dtb_row, ones64, ones128):
    b, s, d = x.shape
    tm = TOKENS_IN
    const = lambda *shape: pl.BlockSpec(shape, lambda bi, si: (0,) * len(shape))
    resident = lambda *shape: pl.BlockSpec(shape, lambda bi, si: (0,) * len(shape), pipeline_mode=pl.Buffered(1))
    tok = lambda width: pl.BlockSpec((1, tm, width), lambda bi, si: (bi, si, 0))
    return pl.pallas_call(
        _mixer_in_kernel,
        grid=(b, s // tm),
        in_specs=[
            tok(d),
            pl.BlockSpec((1, SUBLANES, d), lambda bi, si: (bi, 0, 0)),
            const(1, d),
            resident(*w_main.shape),
            resident(d, LANES),
            const(3, D_CONV),
            const(4, 3 * DN_WIDTH),
            const(1, D_CONV),
            const(1, LANES),
            const(1, LANES),
            const(MXU_COLS, MXU_COLS),
            const(MXU_COLS, MXU_COLS),
        ],
        out_specs=[tok(D_CONV), tok(DN_WIDTH), tok(DN_WIDTH), tok(DN_WIDTH), tok(DN_WIDTH), tok(LANES)],
        out_shape=[
            jax.ShapeDtypeStruct((b, s, D_CONV), BF16),
            jax.ShapeDtypeStruct((b, s, DN_WIDTH), F32),
            jax.ShapeDtypeStruct((b, s, DN_WIDTH), F32),
            jax.ShapeDtypeStruct((b, s, DN_WIDTH), F32),
            jax.ShapeDtypeStruct((b, s, DN_WIDTH), F32),
            jax.ShapeDtypeStruct((b, s, LANES), F32),
        ],
        scratch_shapes=[pltpu.VMEM((SUBLANES, D_CONV), F32), pltpu.VMEM((SUBLANES, 3 * DN_WIDTH), F32)],
        compiler_params=pltpu.CompilerParams(dimension_semantics=("arbitrary", "arbitrary"),
                                             vmem_limit_bytes=VMEM_LIMIT_BYTES),
        name="mixer_in",
    )(x, mod, g_mix, w_main, w_ab, cw_mix, cw_qkv, g_conv_out, alog_row, dtb_row, ones64, ones128)


MASK_CAUSAL = 0
MASK_STRICT = 1
MASK_LEVEL0 = 2
MASK16_CHUNK0 = MASK_LEVEL0 + N_LEVELS


def _pair_masks():
    i = np.arange(PAIR)[:, None]
    j = np.arange(PAIR)[None, :]
    same_chunk = (i // CHUNK) == (j // CHUNK)
    out = [same_chunk & (j <= i), same_chunk & (j < i)]
    for l in range(N_LEVELS):
        s = 1 << l
        out.append((j < i) & ((i // s) != (j // s)) & ((i // (2 * s)) == (j // (2 * s))))
    out += [np.broadcast_to(j // CHUNK == e, (PAIR, PAIR)) for e in range(2)]
    return np.stack(out).astype(np.float32)


def _prep_front(f, base, q_ref, k_ref, v_ref, gb_ref, masks_ref):
    insts = [(p, hd) for p in range(PAIRS_PER_PASS) for hd in range(DN_HEADS)]
    n = range(len(insts))
    wide = (PAIR, DN_HEAD_DIM)
    rows = [slice(base + p * PAIR, base + (p + 1) * PAIR) for p, _ in insts]
    cols = [slice(hd * DN_HEAD_DIM, (hd + 1) * DN_HEAD_DIM) for _, hd in insts]
    f.update(insts=insts, rows=rows, cols=cols)

    def column(i, offset):
        hd = insts[i][1]
        return jnp.broadcast_to(gb_ref[0, rows[i], offset + hd:offset + hd + 1], wide)

    f["q"] = [q_ref[0, rows[i], cols[i]] for i in n]
    f["k"] = k = [k_ref[0, rows[i], cols[i]] for i in n]
    f["v"] = [v_ref[0, rows[i], cols[i]] for i in n]
    f["beta"] = beta = [column(i, GB_BETA) for i in n]
    f["kb"] = [k[i] * beta[i] for i in n]
    yield
    f["exp_gc"] = [column(i, GB_EXP_GC) for i in n]
    f["exp_rest"] = [column(i, GB_EXP_REST) for i in n]
    yield
    causal = masks_ref[MASK_CAUSAL]
    gc = [column(i, GB_GC) for i in n]
    f["decay"] = [jnp.where(causal > 0, jnp.exp(gc[i] - gc[i].T), 0.0) for i in n]
    yield


def _prep_scores(f):
    n = range(len(f["insts"]))
    kq = [_dot_nt(jnp.concatenate([f["kb"][i], f["q"][i]], axis=0).astype(BF16), f["k"][i].astype(BF16)) for i in n]
    f["low"] = [kq[i][:PAIR] * f["decay"][i] for i in n]
    f["attn"] = [(kq[i][PAIR:] * f["decay"][i]).astype(BF16) for i in n]


def _prep_inverse(f, masks_ref, masks16_ref):
    low = f["low"]
    n = range(len(low))
    eye = masks_ref[MASK_CAUSAL] - masks_ref[MASK_STRICT]
    t = [eye - low[i] * masks_ref[MASK_LEVEL0] for i in n]
    low16 = [low[i].astype(BF16) for i in n]
    for l in range(1, N_LEVELS):
        t16 = [t[i].astype(BF16) for i in n]
        x = [_dot(t16[i], low16[i] * masks16_ref[MASK_LEVEL0 + l]) for i in n]
        t = [t[i] - _dot(x[i].astype(BF16), t16[i]) for i in n]
        yield
    f["t"] = t


def _prep_tail(f, base, chunk0, gb_ref, masks16_ref, at_ref, bt_ref, dec_ref, qeff_ref, oin_ref):
    insts, q, k, v, beta, kb, attn, t, exp_gc, exp_rest = (
        f[name] for name in ("insts", "q", "k", "v", "beta", "kb", "attn", "t", "exp_gc", "exp_rest"))
    n = range(len(insts))
    uw = [_dot(t[i].astype(BF16), jnp.concatenate([v[i] * beta[i], kb[i] * exp_gc[i]], axis=1).astype(BF16))
          for i in n]
    yield
    a_uw = [_dot(attn[i], uw[i].astype(BF16)) for i in n]
    k_dec = [(k[i] * exp_rest[i]).astype(BF16) for i in n]
    yield

    ab = []
    for i in n:
        u_t = uw[i][:, :DN_HEAD_DIM].T.astype(BF16)
        w_t = uw[i][:, DN_HEAD_DIM:].T.astype(BF16)
        parts = []
        for e in range(2):
            parts += [w_t * masks16_ref[MASK16_CHUNK0 + e], u_t * masks16_ref[MASK16_CHUNK0 + e]]
        ab.append(_dot(jnp.concatenate(parts, axis=0), k_dec[i]))
    yield

    for i, (p, hd) in enumerate(insts):
        rows, cols = f["rows"][i], f["cols"][i]
        qeff_ref[0, rows, cols] = (q[i] * exp_gc[i] - a_uw[i][:, DN_HEAD_DIM:]).astype(qeff_ref.dtype)
        oin_ref[0, rows, cols] = a_uw[i][:, :DN_HEAD_DIM]
        for e in range(2):
            c = chunk0 + 2 * p + e
            last = base + p * PAIR + (e + 1) * CHUNK - 1
            at_ref[0, hd, c] = ab[i][(2 * e) * DN_HEAD_DIM:(2 * e + 1) * DN_HEAD_DIM].astype(at_ref.dtype)
            bt_ref[0, hd, c] = ab[i][(2 * e + 1) * DN_HEAD_DIM:(2 * e + 2) * DN_HEAD_DIM]
            dec_ref[0, hd, c:c + 1, :] = jnp.broadcast_to(
                gb_ref[0, last:last + 1, GB_EXP_LAST + hd:GB_EXP_LAST + hd + 1], (1, LANES))
    yield


def _run_interleaved(*gens):
    live = list(gens)
    while live:
        live = [g for g in live if next(g, StopIteration) is not StopIteration]


def _delta_prep_kernel(q_ref, k_ref, v_ref, gb_ref, masks_ref, masks16_ref,
                       at_ref, bt_ref, dec_ref, qeff_ref, oin_ref):
    tokens_per_pass = PAIRS_PER_PASS * PAIR
    n_passes = TOKENS_PREP // tokens_per_pass
    outs = (at_ref, bt_ref, dec_ref, qeff_ref, oin_ref)
    fs = [dict() for _ in range(n_passes)]

    def front(i):
        if i >= n_passes:
            return []
        return [_prep_front(fs[i], i * tokens_per_pass, q_ref, k_ref, v_ref, gb_ref, masks_ref)]

    def inverse(i):
        return _prep_inverse(fs[i], masks_ref, masks16_ref)

    def tail(i):
        return _prep_tail(fs[i], i * tokens_per_pass, i * (tokens_per_pass // CHUNK), gb_ref, masks16_ref, *outs)

    _run_interleaved(*front(0))
    _prep_scores(fs[0])
    _run_interleaved(inverse(0), *front(1))
    for i in range(1, n_passes):
        _prep_scores(fs[i])
        _run_interleaved(tail(i - 1), inverse(i), *front(i + 1))
    _run_interleaved(tail(n_passes - 1))


def _delta_prep(q, k, v, gb, masks):
    b, s, _ = q.shape
    tq = TOKENS_PREP
    cpb = tq // CHUNK
    nc = s // CHUNK
    tok = lambda width: pl.BlockSpec((1, tq, width), lambda bi, si: (bi, si, 0))
    mat = pl.BlockSpec((1, DN_HEADS, cpb, DN_HEAD_DIM, DN_HEAD_DIM), lambda bi, si: (bi, 0, si, 0, 0))
    return pl.pallas_call(
        _delta_prep_kernel,
        grid=(b, s // tq),
        in_specs=[tok(DN_WIDTH), tok(DN_WIDTH), tok(DN_WIDTH), tok(LANES),
                  pl.BlockSpec(masks.shape, lambda bi, si: (0, 0, 0)),
                  pl.BlockSpec(masks.shape, lambda bi, si: (0, 0, 0))],
        out_specs=[mat, mat,
                   pl.BlockSpec((1, DN_HEADS, cpb, LANES), lambda bi, si: (bi, 0, si, 0)),
                   tok(DN_WIDTH), tok(DN_WIDTH)],
        out_shape=[
            jax.ShapeDtypeStruct((b, DN_HEADS, nc, DN_HEAD_DIM, DN_HEAD_DIM), BF16),
            jax.ShapeDtypeStruct((b, DN_HEADS, nc, DN_HEAD_DIM, DN_HEAD_DIM), F32),
            jax.ShapeDtypeStruct((b, DN_HEADS, nc, LANES), F32),
            jax.ShapeDtypeStruct((b, s, DN_WIDTH), BF16),
            jax.ShapeDtypeStruct((b, s, DN_WIDTH), F32),
        ],
        compiler_params=pltpu.CompilerParams(dimension_semantics=("arbitrary", "arbitrary"),
                                             vmem_limit_bytes=VMEM_LIMIT_BYTES),
        name="delta_prep",
    )(q, k, v, gb, masks, masks.astype(BF16))


def _delta_scan_kernel(at_ref, bt_ref, dec_ref, qeff_ref, oin_ref, o_ref, state_ref):
    step = pl.program_id(0)

    @pl.when(step == 0)
    def _():
        state_ref[...] = jnp.zeros_like(state_ref)

    nb = at_ref.shape[0]
    chains = [(bi, hd) for bi in range(nb) for hd in range(DN_HEADS)]
    states = [state_ref[bi, hd] for bi, hd in chains]
    for c in range(CHUNKS_SCAN):
        rows = slice(c * CHUNK, (c + 1) * CHUNK)
        for i, (bi, hd) in enumerate(chains):
            cols = slice(hd * DN_HEAD_DIM, (hd + 1) * DN_HEAD_DIM)
            st16 = states[i].astype(BF16)
            o_ref[bi, rows, cols] = oin_ref[bi, rows, cols] + _dot_nt(qeff_ref[bi, rows, cols], st16)
            dec = dec_ref[bi, hd, pl.ds(step * CHUNKS_SCAN + c, 1), :]
            states[i] = states[i] * dec - _dot(st16, at_ref[bi, hd, c]) + bt_ref[bi, hd, c]
    for i, (bi, hd) in enumerate(chains):
        state_ref[bi, hd] = states[i]


def _delta_scan(at, bt, dec, qeff, oin):
    b, s, _ = qeff.shape
    nc = s // CHUNK
    tq = CHUNKS_SCAN * CHUNK
    mat = pl.BlockSpec((b, DN_HEADS, CHUNKS_SCAN, DN_HEAD_DIM, DN_HEAD_DIM), lambda si: (0, 0, si, 0, 0))
    tok = pl.BlockSpec((b, tq, DN_WIDTH), lambda si: (0, si, 0))
    return pl.pallas_call(
        _delta_scan_kernel,
        grid=(nc // CHUNKS_SCAN,),
        in_specs=[mat, mat, pl.BlockSpec((b, DN_HEADS, nc, LANES), lambda si: (0, 0, 0, 0)), tok, tok],
        out_specs=tok,
        out_shape=jax.ShapeDtypeStruct((b, s, DN_WIDTH), F32),
        scratch_shapes=[pltpu.VMEM((b, DN_HEADS, DN_HEAD_DIM, DN_HEAD_DIM), F32)],
        compiler_params=pltpu.CompilerParams(dimension_semantics=("arbitrary",),
                                             vmem_limit_bytes=VMEM_LIMIT_BYTES),
        name="delta_scan",
    )(at, bt, dec, qeff, oin)


def _mixer_out_kernel(x_ref, ya_ref, o_ref, z_ref, mod_ref, gdn_ref, ones128_ref, wout_ref, gffn_ref, wgu_ref,
                      wd_ref, gfin_ref, out_ref):
    x = x_ref[0]
    mod = mod_ref[0]
    o = o_ref[0]
    ms = _segment_sum(o * o, ones128_ref[...]) * (1.0 / DN_HEAD_DIM)
    yb = ((o * lax.rsqrt(ms + EPS)) * gdn_ref[...]) * _silu(z_ref[0])
    y = _dot(ya_ref[0], wout_ref[:D_CONV, :]) + _dot(yb.astype(BF16), wout_ref[D_CONV:, :])
    x = x + mod[2:3] * y

    h = _adaln(x, gffn_ref[...], mod[3:4], mod[4:5]).astype(BF16)
    acc = None
    for j in range(FFN_HIDDEN // FFN_COLS):
        gate = _dot(h, wgu_ref[:, j * FFN_COLS:(j + 1) * FFN_COLS])
        up = _dot(h, wgu_ref[:, FFN_HIDDEN + j * FFN_COLS:FFN_HIDDEN + (j + 1) * FFN_COLS])
        part = _dot((_silu(gate) * up).astype(BF16), wd_ref[j * FFN_COLS:(j + 1) * FFN_COLS, :])
        acc = part if acc is None else acc + part
    x = x + mod[5:6] * acc
    out_ref[0] = _adaln(x, gfin_ref[...], mod[6:7], mod[7:8])


def _mixer_out(x, ya, o, z, mod, g_dn, ones128, w_out, g_ffn, w_gate_up, w_down, g_final):
    b, s, d = x.shape
    tm = TOKENS_OUT
    const = lambda *shape: pl.BlockSpec(shape, lambda bi, si: (0,) * len(shape))
    resident = lambda *shape: pl.BlockSpec(shape, lambda bi, si: (0,) * len(shape), pipeline_mode=pl.Buffered(1))
    tok = lambda width: pl.BlockSpec((1, tm, width), lambda bi, si: (bi, si, 0))
    return pl.pallas_call(
        _mixer_out_kernel,
        grid=(b, s // tm),
        in_specs=[
            tok(d), tok(D_CONV), tok(DN_WIDTH), tok(DN_WIDTH),
            pl.BlockSpec((1, SUBLANES, d), lambda bi, si: (bi, 0, 0)),
            const(1, DN_WIDTH),
            const(MXU_COLS, MXU_COLS),
            resident(d, d),
            const(1, d),
            resident(d, 2 * FFN_HIDDEN),
            resident(FFN_HIDDEN, d),
            const(1, d),
        ],
        out_specs=tok(d),
        out_shape=jax.ShapeDtypeStruct((b, s, d), F32),
        compiler_params=pltpu.CompilerParams(dimension_semantics=("arbitrary", "arbitrary"),
                                             vmem_limit_bytes=VMEM_LIMIT_BYTES),
        name="mixer_out",
    )(x, ya, o, z, mod, g_dn, ones128, w_out, g_ffn, w_gate_up, w_down, g_final)


def _block_ones(n, block):
    i = np.arange(n)
    return jnp.asarray((i[:, None] // block) == (i[None, :] // block), dtype=BF16)


def _gb_lanes(decay_cols, beta_cols):
    groups = {GB_GC: decay_cols, GB_BETA: beta_cols, GB_EXP_GC: decay_cols, GB_EXP_REST: decay_cols,
              GB_EXP_LAST: decay_cols}
    used = [groups[off] for off in sorted(groups)]
    lead = decay_cols.shape[:-1]
    return jnp.concatenate(used + [jnp.zeros(lead + (LANES - len(used) * DN_HEADS,), decay_cols.dtype)], axis=-1)


def kernel(x, c, w_ada, b_ada, w_ada_final, b_ada_final, g_norm_mix, g_norm_ffn, g_norm_final, w_in, conv_w_mix,
           conv_w_qkv, a_log, dt_bias, g_conv_out, g_dn_out, w_out, w_gate_up, w_down):
    b, s, d = x.shape
    assert d == D_MODEL and w_ada.shape[0] == 1, "one layer of width D_MODEL"
    assert s % TOKENS_IN == 0 and s % TOKENS_PREP == 0 and s % TOKENS_OUT == 0
    assert TOKENS_PREP % (SUBLANES * CHUNK) == 0 and (s // CHUNK) % CHUNKS_SCAN == 0

    c_pad = jnp.pad(c, ((0, (-b) % SUBLANES), (0, 0)))
    mod = jnp.concatenate([
        _modulation(c_pad, w_ada[0], b_ada[0].reshape(1, -1)),
        _modulation(c_pad, w_ada_final, b_ada_final.reshape(1, -1)),
    ], axis=1)[:b].reshape(b, SUBLANES, d)

    w_main = w_in[0].astype(BF16)
    w_ab = _gb_lanes(w_main[:, N_MAIN:N_MAIN + DN_HEADS], w_main[:, N_MAIN + DN_HEADS:])
    no_beta = jnp.zeros((1, DN_HEADS), F32)
    alog_row = _gb_lanes(a_log[0].astype(F32).reshape(1, DN_HEADS), no_beta)
    dtb_row = _gb_lanes(dt_bias[0].astype(F32).reshape(1, DN_HEADS), no_beta)
    ones64 = _block_ones(MXU_COLS, CONV_GROUP_DIM)
    ones128 = _block_ones(MXU_COLS, DN_HEAD_DIM)

    ya, q, k, v, z, gb = _mixer_in(
        x, mod, g_norm_mix[0].reshape(1, d), w_main, w_ab, conv_w_mix[0], conv_w_qkv[0],
        g_conv_out[0].reshape(1, D_CONV), alog_row, dtb_row, ones64, ones128)

    o = _delta_scan(*_delta_prep(q, k, v, gb, jnp.asarray(_pair_masks())))

    return _mixer_out(
        x, ya, o, z, mod, jnp.tile(g_dn_out[0], DN_HEADS).reshape(1, DN_WIDTH), ones128,
        w_out[0].astype(BF16), g_norm_ffn[0].reshape(1, d), w_gate_up[0].astype(BF16), w_down[0].astype(BF16),
        g_norm_final.reshape(1, d))
```

```python
import numpy as np
import jax
import jax.numpy as jnp
from jax import lax
from jax.experimental import pallas as pl
from jax.experimental.pallas import tpu as pltpu

F32 = jnp.float32
BF16 = jnp.bfloat16

D_MODEL = 1024
D_CONV = 512
CONV_GROUP_DIM = 64
DN_HEADS = 4
DN_HEAD_DIM = 128
DN_WIDTH = DN_HEADS * DN_HEAD_DIM
CHUNK = 64
FFN_HIDDEN = 2816
EPS = 1e-6

LANES = 128
SUBLANES = 8
MXU_COLS = 256
N_MAIN = 3 * D_CONV + 4 * DN_WIDTH
PAIR = 2 * CHUNK
N_LEVELS = 6
GB_GC = 0 * DN_HEADS
GB_BETA = 1 * DN_HEADS
GB_EXP_GC = 2 * DN_HEADS
GB_EXP_REST = 3 * DN_HEADS
GB_EXP_LAST = 4 * DN_HEADS
TOKENS_IN = 1024
TOKENS_PREP = 1024
PAIRS_PER_PASS = 4
CHUNKS_SCAN = 4
TOKENS_OUT = 1024
FFN_COLS = 256
MOD_COLS = 1024
VMEM_LIMIT_BYTES = 56 * 1024 * 1024


def _dot(a, b):
    return jnp.dot(a, b, preferred_element_type=F32)


def _dot_nt(a, b):
    return lax.dot_general(a, b, (((1,), (1,)), ((), ())), preferred_element_type=F32)


def _split2(x):
    hi = x.astype(BF16)
    lo = (x - hi.astype(F32)).astype(BF16)
    return hi, lo


def _dot3(a2, b2):
    return _dot(a2[0], b2[0]) + (_dot(a2[0], b2[1]) + _dot(a2[1], b2[0]))


def _segment_sum(x, ones_bf16):
    xb = x.astype(BF16)
    w = ones_bf16.shape[0]
    return jnp.concatenate([_dot(xb[:, j:j + w], ones_bf16) for j in range(0, x.shape[1], w)], axis=1)


def _sigmoid(x):
    return 1.0 / (1.0 + jnp.exp(-x))


def _silu(x):
    return x * _sigmoid(x)


def _adaln(x, gain, shift, scale):
    ms = jnp.mean(x * x, axis=-1, keepdims=True)
    return (x * lax.rsqrt(ms + EPS)) * gain * (1.0 + scale) + shift


def _mod_kernel(c_ref, w_ref, b_ref, o_ref):
    c = c_ref[...]
    ca = _split2(_silu(c))
    o_ref[...] = _dot3(ca, _split2(w_ref[...])) + b_ref[...]


def _modulation(c_pad, w, b):
    rows, d = c_pad.shape
    n = w.shape[1]
    return pl.pallas_call(
        _mod_kernel,
        grid=(n // MOD_COLS,),
        in_specs=[
            pl.BlockSpec((rows, d), lambda j: (0, 0)),
            pl.BlockSpec((d, MOD_COLS), lambda j: (0, j)),
            pl.BlockSpec((1, MOD_COLS), lambda j: (0, j)),
        ],
        out_specs=pl.BlockSpec((rows, MOD_COLS), lambda j: (0, j)),
        out_shape=jax.ShapeDtypeStruct((rows, n), F32),
        compiler_params=pltpu.CompilerParams(dimension_semantics=("arbitrary",)),
        name="modulation",
    )(c_pad, w, b)


def _shift_rows(u, tail, j):
    r = pltpu.roll(u, j, 0)
    rt = pltpu.roll(tail, j, 0)
    rows = lax.broadcasted_iota(jnp.int32, tail.shape, 0)
    head = jnp.where(rows < j, rt, r[:SUBLANES])
    return jnp.concatenate([head, r[SUBLANES:]], axis=0)


def _causal_conv(u, tail, w):
    k = w.shape[0]
    out = u * w[k - 1:k]
    for j in range(1, k):
        out = out + _shift_rows(u, tail, j) * w[k - 1 - j:k - j]
    return out


def _mixer_in_kernel(x_ref, mod_ref, gmix_ref, w_ref, wab_ref, cwm_ref, cwq_ref, gco_ref, alog_ref, dtb_ref,
                     ones64_ref, ones128_ref,
                     ya_ref, q_ref, k_ref, v_ref, z_ref, gb_ref,
                     tail_u, tail_qkv):
    @pl.when(pl.program_id(1) == 0)
    def _():
        tail_u[...] = jnp.zeros_like(tail_u)
        tail_qkv[...] = jnp.zeros_like(tail_qkv)

    x = x_ref[0]
    mod = mod_ref[0]
    h = _adaln(x, gmix_ref[...], mod[0:1], mod[1:2]).astype(BF16)

    def proj(i):
        return _dot(h, w_ref[:, i * D_CONV:(i + 1) * D_CONV])

    pre_qkv = [proj(3 + i) for i in range(3)]
    gate_c, h_conv, gate_b = proj(1), proj(2), proj(0)
    ab = _dot(h, wab_ref[...])

    tails = tail_qkv[...]
    cwq = cwq_ref[...]
    act = []
    for i in range(3):
        cols = slice(i * DN_WIDTH, (i + 1) * DN_WIDTH)
        act.append(_silu(_causal_conv(pre_qkv[i], tails[:, cols], cwq[:, cols])))
        tail_qkv[:, cols] = pre_qkv[i][-SUBLANES:]
    v_ref[0] = act[2].astype(v_ref.dtype)
    for i, ref in enumerate((q_ref, k_ref)):
        ss = _segment_sum(act[i] * act[i], ones128_ref[...])
        scale = DN_HEAD_DIM ** -0.5 if i == 0 else 1.0
        ref[0] = (act[i] * lax.rsqrt(ss + EPS) * scale).astype(ref.dtype)
    z_ref[0] = proj(6).astype(z_ref.dtype)

    u = gate_c * h_conv
    ya = gate_b * _causal_conv(u, tail_u[...], cwm_ref[...])
    tail_u[...] = u[-SUBLANES:]
    ms = _segment_sum(ya * ya, ones64_ref[...]) * (1.0 / CONV_GROUP_DIM)
    ya_ref[0] = ((ya * lax.rsqrt(ms + EPS)) * gco_ref[...]).astype(ya_ref.dtype)

    sp_in = ab + dtb_ref[...]
    softplus = jnp.maximum(sp_in, 0.0) + jnp.log1p(jnp.exp(-jnp.abs(sp_in)))
    g = -jnp.exp(alog_ref[...]) * softplus
    row_in_chunk = lax.broadcasted_iota(jnp.int32, ab.shape, 0) & (CHUNK - 1)
    gc = g
    for l in range(N_LEVELS):
        s = 1 << l
        gc = gc + jnp.where(row_in_chunk >= s, pltpu.roll(gc, s, 0), 0.0)
    g_last = jnp.concatenate(
        [jnp.broadcast_to(gc[c * CHUNK + CHUNK - 1:(c + 1) * CHUNK], (CHUNK, LANES)) for c in range(ab.shape[0] // CHUNK)],
        axis=0)
    lane = lax.broadcasted_iota(jnp.int32, ab.shape, 1)
    gb_ref[0] = jnp.where(
        lane < GB_BETA, gc, jnp.where(
            lane < GB_EXP_GC, _sigmoid(ab), jnp.where(
                lane < GB_EXP_REST, jnp.exp(gc), jnp.where(
                    lane < GB_EXP_LAST, jnp.exp(g_last - gc), jnp.exp(g_last)))))


def _mixer_in(x, mod, g_mix, w_main, w_ab, cw_mix, cw_qkv, g_conv_out, alog_row, dtb_row, ones64, ones128):
    b, s, d = x.shape
    tm = TOKENS_IN
    const = lambda *shape: pl.BlockSpec(shape, lambda bi, si: (0,) * len(shape))
    resident = lambda *shape: pl.BlockSpec(shape, lambda bi, si: (0,) * len(shape), pipeline_mode=pl.Buffered(1))
    tok = lambda width: pl.BlockSpec((1, tm, width), lambda bi, si: (bi, si, 0))
    return pl.pallas_call(
        _mixer_in_kernel,
        grid=(b, s // tm),
        in_specs=[
            tok(d),
            pl.BlockSpec((1, SUBLANES, d), lambda bi, si: (bi, 0, 0)),
            const(1, d),
            resident(*w_main.shape),
            resident(d, LANES),
            const(3, D_CONV),
            const(4, 3 * DN_WIDTH),
            const(1, D_CONV),
            const(1, LANES),
            const(1, LANES),
            const(MXU_COLS, MXU_COLS),
            const(MXU_COLS, MXU_COLS),
        ],
        out_specs=[tok(D_CONV), tok(DN_WIDTH), tok(DN_WIDTH), tok(DN_WIDTH), tok(DN_WIDTH), tok(LANES)],
        out_shape=[
            jax.ShapeDtypeStruct((b, s, D_CONV), BF16),
            jax.ShapeDtypeStruct((b, s, DN_WIDTH), BF16),
            jax.ShapeDtypeStruct((b, s, DN_WIDTH), BF16),
            jax.ShapeDtypeStruct((b, s, DN_WIDTH), BF16),
            jax.ShapeDtypeStruct((b, s, DN_WIDTH), BF16),
            jax.ShapeDtypeStruct((b, s, LANES), F32),
        ],
        scratch_shapes=[pltpu.VMEM((SUBLANES, D_CONV), F32), pltpu.VMEM((SUBLANES, 3 * DN_WIDTH), F32)],
        compiler_params=pltpu.CompilerParams(dimension_semantics=("arbitrary", "arbitrary"),
                                             vmem_limit_bytes=VMEM_LIMIT_BYTES),
        name="mixer_in",
    )(x, mod, g_mix, w_main, w_ab, cw_mix, cw_qkv, g_conv_out, alog_row, dtb_row, ones64, ones128)


MASK_CAUSAL = 0
MASK_STRICT = 1
MASK_LEVEL0 = 2
MASK16_CHUNK0 = MASK_LEVEL0 + N_LEVELS


def _pair_masks():
    i = np.arange(PAIR)[:, None]
    j = np.arange(PAIR)[None, :]
    same_chunk = (i // CHUNK) == (j // CHUNK)
    out = [same_chunk & (j <= i), same_chunk & (j < i)]
    for l in range(N_LEVELS):
        s = 1 << l
        out.append((j < i) & ((i // s) != (j // s)) & ((i // (2 * s)) == (j // (2 * s))))
    out += [np.broadcast_to(j // CHUNK == e, (PAIR, PAIR)) for e in range(2)]
    return np.stack(out).astype(np.float32)


def _prep_front(f, base, q_ref, k_ref, v_ref, gb_ref, masks_ref):
    insts = [(p, hd) for p in range(PAIRS_PER_PASS) for hd in range(DN_HEADS)]
    n = range(len(insts))
    wide = (PAIR, DN_HEAD_DIM)
    rows = [slice(base + p * PAIR, base + (p + 1) * PAIR) for p, _ in insts]
    cols = [slice(hd * DN_HEAD_DIM, (hd + 1) * DN_HEAD_DIM) for _, hd in insts]
    f.update(insts=insts, rows=rows, cols=cols)

    def column(i, offset):
        hd = insts[i][1]
        return jnp.broadcast_to(gb_ref[0, rows[i], offset + hd:offset + hd + 1], wide)

    f["q"] = [q_ref[0, rows[i], cols[i]].astype(F32) for i in n]
    f["k"] = k = [k_ref[0, rows[i], cols[i]].astype(F32) for i in n]
    f["v"] = [v_ref[0, rows[i], cols[i]].astype(F32) for i in n]
    f["beta"] = beta = [column(i, GB_BETA) for i in n]
    f["kb"] = [k[i] * beta[i] for i in n]
    yield
    f["exp_gc"] = [column(i, GB_EXP_GC) for i in n]
    f["exp_rest"] = [column(i, GB_EXP_REST) for i in n]
    yield
    causal = masks_ref[MASK_CAUSAL]
    gc = [column(i, GB_GC) for i in n]
    f["decay"] = [jnp.where(causal > 0, jnp.exp(gc[i] - gc[i].T), 0.0) for i in n]
    yield


def _prep_scores(f):
    n = range(len(f["insts"]))
    kq = [_dot_nt(jnp.concatenate([f["kb"][i], f["q"][i]], axis=0).astype(BF16), f["k"][i].astype(BF16)) for i in n]
    f["low"] = [kq[i][:PAIR] * f["decay"][i] for i in n]
    f["attn"] = [(kq[i][PAIR:] * f["decay"][i]).astype(BF16) for i in n]


def _prep_inverse(f, masks_ref, masks16_ref):
    low = f["low"]
    n = range(len(low))
    eye = masks_ref[MASK_CAUSAL] - masks_ref[MASK_STRICT]
    t = [eye - low[i] * masks_ref[MASK_LEVEL0] for i in n]
    low16 = [low[i].astype(BF16) for i in n]
    for l in range(1, N_LEVELS):
        t16 = [t[i].astype(BF16) for i in n]
        x = [_dot(t16[i], low16[i] * masks16_ref[MASK_LEVEL0 + l]) for i in n]
        t = [t[i] - _dot(x[i].astype(BF16), t16[i]) for i in n]
        yield
    f["t"] = t


def _prep_tail(f, base, chunk0, gb_ref, masks16_ref, at_ref, bt_ref, dec_ref, qeff_ref, oin_ref):
    insts, q, k, v, beta, kb, attn, t, exp_gc, exp_rest = (
        f[name] for name in ("insts", "q", "k", "v", "beta", "kb", "attn", "t", "exp_gc", "exp_rest"))
    n = range(len(insts))
    uw = [_dot(t[i].astype(BF16), jnp.concatenate([v[i] * beta[i], kb[i] * exp_gc[i]], axis=1).astype(BF16))
          for i in n]
    yield
    a_uw = [_dot(attn[i], uw[i].astype(BF16)) for i in n]
    k_dec = [(k[i] * exp_rest[i]).astype(BF16) for i in n]
    yield

    ab = []
    for i in n:
        u_t = uw[i][:, :DN_HEAD_DIM].T.astype(BF16)
        w_t = uw[i][:, DN_HEAD_DIM:].T.astype(BF16)
        parts = []
        for e in range(2):
            parts += [w_t * masks16_ref[MASK16_CHUNK0 + e], u_t * masks16_ref[MASK16_CHUNK0 + e]]
        ab.append(_dot(jnp.concatenate(parts, axis=0), k_dec[i]))
    yield

    for i, (p, hd) in enumerate(insts):
        rows, cols = f["rows"][i], f["cols"][i]
        qeff_ref[0, rows, cols] = (q[i] * exp_gc[i] - a_uw[i][:, DN_HEAD_DIM:]).astype(qeff_ref.dtype)
        oin_ref[0, rows, cols] = a_uw[i][:, :DN_HEAD_DIM].astype(oin_ref.dtype)
        for e in range(2):
            c = chunk0 + 2 * p + e
            last = base + p * PAIR + (e + 1) * CHUNK - 1
            at_ref[0, hd, c] = ab[i][(2 * e) * DN_HEAD_DIM:(2 * e + 1) * DN_HEAD_DIM].astype(at_ref.dtype)
            bt_ref[0, hd, c] = ab[i][(2 * e + 1) * DN_HEAD_DIM:(2 * e + 2) * DN_HEAD_DIM].astype(bt_ref.dtype)
            dec_ref[0, hd, c:c + 1, :] = jnp.broadcast_to(
                gb_ref[0, last:last + 1, GB_EXP_LAST + hd:GB_EXP_LAST + hd + 1], (1, LANES))
    yield


def _run_interleaved(*gens):
    live = list(gens)
    while live:
        live = [g for g in live if next(g, StopIteration) is not StopIteration]


def _delta_prep_kernel(q_ref, k_ref, v_ref, gb_ref, masks_ref, masks16_ref,
                       at_ref, bt_ref, dec_ref, qeff_ref, oin_ref):
    tokens_per_pass = PAIRS_PER_PASS * PAIR
    n_passes = TOKENS_PREP // tokens_per_pass
    outs = (at_ref, bt_ref, dec_ref, qeff_ref, oin_ref)
    fs = [dict() for _ in range(n_passes)]

    def front(i):
        if i >= n_passes:
            return []
        return [_prep_front(fs[i], i * tokens_per_pass, q_ref, k_ref, v_ref, gb_ref, masks_ref)]

    def inverse(i):
        return _prep_inverse(fs[i], masks_ref, masks16_ref)

    def tail(i):
        return _prep_tail(fs[i], i * tokens_per_pass, i * (tokens_per_pass // CHUNK), gb_ref, masks16_ref, *outs)

    _run_interleaved(*front(0))
    _prep_scores(fs[0])
    _run_interleaved(inverse(0), *front(1))
    for i in range(1, n_passes):
        _prep_scores(fs[i])
        _run_interleaved(tail(i - 1), inverse(i), *front(i + 1))
    _run_interleaved(tail(n_passes - 1))


def _delta_prep(q, k, v, gb, masks):
    b, s, _ = q.shape
    tq = TOKENS_PREP
    cpb = tq // CHUNK
    nc = s // CHUNK
    tok = lambda width: pl.BlockSpec((1, tq, width), lambda bi, si: (bi, si, 0))
    mat = pl.BlockSpec((1, DN_HEADS, cpb, DN_HEAD_DIM, DN_HEAD_DIM), lambda bi, si: (bi, 0, si, 0, 0))
    return pl.pallas_call(
        _delta_prep_kernel,
        grid=(b, s // tq),
        in_specs=[tok(DN_WIDTH), tok(DN_WIDTH), tok(DN_WIDTH), tok(LANES),
                  pl.BlockSpec(masks.shape, lambda bi, si: (0, 0, 0)),
                  pl.BlockSpec(masks.shape, lambda bi, si: (0, 0, 0))],
        out_specs=[mat, mat,
                   pl.BlockSpec((1, DN_HEADS, cpb, LANES), lambda bi, si: (bi, 0, si, 0)),
                   tok(DN_WIDTH), tok(DN_WIDTH)],
        out_shape=[
            jax.ShapeDtypeStruct((b, DN_HEADS, nc, DN_HEAD_DIM, DN_HEAD_DIM), BF16),
            jax.ShapeDtypeStruct((b, DN_HEADS, nc, DN_HEAD_DIM, DN_HEAD_DIM), BF16),
            jax.ShapeDtypeStruct((b, DN_HEADS, nc, LANES), F32),
            jax.ShapeDtypeStruct((b, s, DN_WIDTH), BF16),
            jax.ShapeDtypeStruct((b, s, DN_WIDTH), BF16),
        ],
        compiler_params=pltpu.CompilerParams(dimension_semantics=("arbitrary", "arbitrary"),
                                             vmem_limit_bytes=VMEM_LIMIT_BYTES),
        name="delta_prep",
    )(q, k, v, gb, masks, masks.astype(BF16))


def _delta_scan_kernel(at_ref, bt_ref, dec_ref, qeff_ref, oin_ref, o_ref, state_ref):
    step = pl.program_id(0)

    @pl.when(step == 0)
    def _():
        state_ref[...] = jnp.zeros_like(state_ref)

    nb = at_ref.shape[0]
    chains = [(bi, hd) for bi in range(nb) for hd in range(DN_HEADS)]
    states = [state_ref[bi, hd] for bi, hd in chains]
    for c in range(CHUNKS_SCAN):
        rows = slice(c * CHUNK, (c + 1) * CHUNK)
        for i, (bi, hd) in enumerate(chains):
            cols = slice(hd * DN_HEAD_DIM, (hd + 1) * DN_HEAD_DIM)
            st16 = states[i].astype(BF16)
            o = oin_ref[bi, rows, cols].astype(F32) + _dot_nt(qeff_ref[bi, rows, cols], st16)
            o_ref[bi, rows, cols] = o.astype(o_ref.dtype)
            dec = dec_ref[bi, hd, pl.ds(step * CHUNKS_SCAN + c, 1), :]
            states[i] = states[i] * dec - _dot(st16, at_ref[bi, hd, c]) + bt_ref[bi, hd, c].astype(F32)
    for i, (bi, hd) in enumerate(chains):
        state_ref[bi, hd] = states[i]


def _delta_scan(at, bt, dec, qeff, oin):
    b, s, _ = qeff.shape
    nc = s // CHUNK
    tq = CHUNKS_SCAN * CHUNK
    mat = pl.BlockSpec((b, DN_HEADS, CHUNKS_SCAN, DN_HEAD_DIM, DN_HEAD_DIM), lambda si: (0, 0, si, 0, 0))
    tok = pl.BlockSpec((b, tq, DN_WIDTH), lambda si: (0, si, 0))
    return pl.pallas_call(
        _delta_scan_kernel,
        grid=(nc // CHUNKS_SCAN,),
        in_specs=[mat, mat, pl.BlockSpec((b, DN_HEADS, nc, LANES), lambda si: (0, 0, 0, 0)), tok, tok],
        out_specs=tok,
        out_shape=jax.ShapeDtypeStruct((b, s, DN_WIDTH), BF16),
        scratch_shapes=[pltpu.VMEM((b, DN_HEADS, DN_HEAD_DIM, DN_HEAD_DIM), F32)],
        compiler_params=pltpu.CompilerParams(dimension_semantics=("arbitrary",),
                                             vmem_limit_bytes=VMEM_LIMIT_BYTES),
        name="delta_scan",
    )(at, bt, dec, qeff, oin)


def _mixer_out_kernel(x_ref, ya_ref, o_ref, z_ref, mod_ref, gdn_ref, ones128_ref, wout_ref, gffn_ref, wgu_ref,
                      wd_ref, gfin_ref, out_ref):
    x = x_ref[0]
    mod = mod_ref[0]
    y_a = _dot(ya_ref[0], wout_ref[:D_CONV, :])
    o = o_ref[0].astype(F32)
    ms = _segment_sum(o * o, ones128_ref[...]) * (1.0 / DN_HEAD_DIM)
    yb = ((o * lax.rsqrt(ms + EPS)) * gdn_ref[...]) * _silu(z_ref[0].astype(F32))
    y = y_a + _dot(yb.astype(BF16), wout_ref[D_CONV:, :])
    x = x + mod[2:3] * y

    h = _adaln(x, gffn_ref[...], mod[3:4], mod[4:5]).astype(BF16)
    acc = None
    for j in range(FFN_HIDDEN // FFN_COLS):
        gate = _dot(h, wgu_ref[:, j * FFN_COLS:(j + 1) * FFN_COLS])
        up = _dot(h, wgu_ref[:, FFN_HIDDEN + j * FFN_COLS:FFN_HIDDEN + (j + 1) * FFN_COLS])
        part = _dot((_silu(gate) * up).astype(BF16), wd_ref[j * FFN_COLS:(j + 1) * FFN_COLS, :])
        acc = part if acc is None else acc + part
    x = x + mod[5:6] * acc
    out_ref[0] = _adaln(x, gfin_ref[...], mod[6:7], mod[7:8])


def _mixer_out(x, ya, o, z, mod, g_dn, ones128, w_out, g_ffn, w_gate_up, w_down, g_final):
    b, s, d = x.shape
    tm = TOKENS_OUT
    const = lambda *shape: pl.BlockSpec(shape, lambda bi, si: (0,) * len(shape))
    resident = lambda *shape: pl.BlockSpec(shape, lambda bi, si: (0,) * len(shape), pipeline_mode=pl.Buffered(1))
    tok = lambda width: pl.BlockSpec((1, tm, width), lambda bi, si: (bi, si, 0))
    return pl.pallas_call(
        _mixer_out_kernel,
        grid=(b, s // tm),
        in_specs=[
            tok(d), tok(D_CONV), tok(DN_WIDTH), tok(DN_WIDTH),
            pl.BlockSpec((1, SUBLANES, d), lambda bi, si: (bi, 0, 0)),
            const(1, DN_WIDTH),
            const(MXU_COLS, MXU_COLS),
            resident(d, d),
            const(1, d),
            resident(d, 2 * FFN_HIDDEN),
            resident(FFN_HIDDEN, d),
            const(1, d),
        ],
        out_specs=tok(d),
        out_shape=jax.ShapeDtypeStruct((b, s, d), F32),
        compiler_params=pltpu.CompilerParams(dimension_semantics=("arbitrary", "arbitrary"),
                                             vmem_limit_bytes=VMEM_LIMIT_BYTES),
        name="mixer_out",
    )(x, ya, o, z, mod, g_dn, ones128, w_out, g_ffn, w_gate_up, w_down, g_final)


def _block_ones(n, block):
    i = np.arange(n)
    return jnp.asarray((i[:, None] // block) == (i[None, :] // block), dtype=BF16)


def _gb_lanes(decay_cols, beta_cols):
    groups = {GB_GC: decay_cols, GB_BETA: beta_cols, GB_EXP_GC: decay_cols, GB_EXP_REST: decay_cols,
              GB_EXP_LAST: decay_cols}
    used = [groups[off] for off in sorted(groups)]
    lead = decay_cols.shape[:-1]
    return jnp.concatenate(used + [jnp.zeros(lead + (LANES - len(used) * DN_HEADS,), decay_cols.dtype)], axis=-1)


def kernel(x, c, w_ada, b_ada, w_ada_final, b_ada_final, g_norm_mix, g_norm_ffn, g_norm_final, w_in, conv_w_mix,
           conv_w_qkv, a_log, dt_bias, g_conv_out, g_dn_out, w_out, w_gate_up, w_down):
    b, s, d = x.shape
    assert d == D_MODEL and w_ada.shape[0] == 1, "one layer of width D_MODEL"
    assert s % TOKENS_IN == 0 and s % TOKENS_PREP == 0 and s % TOKENS_OUT == 0
    assert TOKENS_PREP % (SUBLANES * CHUNK) == 0 and (s // CHUNK) % CHUNKS_SCAN == 0

    c_pad = jnp.pad(c, ((0, (-b) % SUBLANES), (0, 0)))
    mod = jnp.concatenate([
        _modulation(c_pad, w_ada[0], b_ada[0].reshape(1, -1)),
        _modulation(c_pad, w_ada_final, b_ada_final.reshape(1, -1)),
    ], axis=1)[:b].reshape(b, SUBLANES, d)

    w_main = w_in[0].astype(BF16)
    w_ab = _gb_lanes(w_main[:, N_MAIN:N_MAIN + DN_HEADS], w_main[:, N_MAIN + DN_HEADS:])
    no_beta = jnp.zeros((1, DN_HEADS), F32)
    alog_row = _gb_lanes(a_log[0].astype(F32).reshape(1, DN_HEADS), no_beta)
    dtb_row = _gb_lanes(dt_bias[0].astype(F32).reshape(1, DN_HEADS), no_beta)
    ones64 = _block_ones(MXU_COLS, CONV_GROUP_DIM)
    ones128 = _block_ones(MXU_COLS, DN_HEAD_DIM)

    ya, q, k, v, z, gb = _mixer_in(
        x, mod, g_norm_mix[0].reshape(1, d), w_main, w_ab, conv_w_mix[0], conv_w_qkv[0],
        g_conv_out[0].reshape(1, D_CONV), alog_row, dtb_row, ones64, ones128)

    o = _delta_scan(*_delta_prep(q, k, v, gb, jnp.asarray(_pair_masks())))

    return _mixer_out(
        x, ya, o, z, mod, jnp.tile(g_dn_out[0], DN_HEADS).reshape(1, DN_WIDTH), ones128,
        w_out[0].astype(BF16), g_norm_ffn[0].reshape(1, d), w_gate_up[0].astype(BF16), w_down[0].astype(BF16),
        g_norm_final.reshape(1, d))
```

```python
import numpy as np
import jax
import jax.numpy as jnp
from jax import lax
from jax.experimental import pallas as pl
from jax.experimental.pallas import tpu as pltpu

F32 = jnp.float32
BF16 = jnp.bfloat16

D_MODEL = 1024
D_CONV = 512
CONV_GROUP_DIM = 64
DN_HEADS = 4
DN_HEAD_DIM = 128
DN_WIDTH = DN_HEADS * DN_HEAD_DIM
CHUNK = 64
FFN_HIDDEN = 2816
EPS = 1e-6

LANES = 128
SUBLANES = 8
MXU_COLS = 256
N_MAIN = 3 * D_CONV + 4 * DN_WIDTH
PAIR = 2 * CHUNK
N_LEVELS = 6
GB_GC = 0 * DN_HEADS
GB_BETA = 1 * DN_HEADS
GB_EXP_GC = 2 * DN_HEADS
GB_EXP_REST = 3 * DN_HEADS
GB_EXP_LAST = 4 * DN_HEADS
TOKENS_IN = 1024
TOKENS_PREP = 1024
PAIRS_PER_PASS = 4
CHUNKS_SCAN = 4
TOKENS_OUT = 1024
ROWS_PER_GROUP = 512
FFN_COLS = 256
MOD_COLS = 1024
VMEM_LIMIT_BYTES = 56 * 1024 * 1024


def _dot(a, b):
    return jnp.dot(a, b, preferred_element_type=F32)


def _dot_nt(a, b):
    return lax.dot_general(a, b, (((1,), (1,)), ((), ())), preferred_element_type=F32)


def _split2(x):
    hi = x.astype(BF16)
    lo = (x - hi.astype(F32)).astype(BF16)
    return hi, lo


def _dot3(a2, b2):
    return _dot(a2[0], b2[0]) + (_dot(a2[0], b2[1]) + _dot(a2[1], b2[0]))


def _segment_sum(x, ones_bf16):
    xb = x.astype(BF16)
    w = ones_bf16.shape[0]
    return jnp.concatenate([_dot(xb[:, j:j + w], ones_bf16) for j in range(0, x.shape[1], w)], axis=1)


def _sigmoid(x):
    return 1.0 / (1.0 + jnp.exp(-x))


def _silu(x):
    return x * _sigmoid(x)


def _adaln(x, gain, shift, scale):
    ms = jnp.mean(x * x, axis=-1, keepdims=True)
    return (x * lax.rsqrt(ms + EPS)) * (gain * (1.0 + scale)) + shift


def _mod_kernel(c_ref, w_ref, b_ref, o_ref):
    c = c_ref[...]
    ca = _split2(_silu(c))
    o_ref[...] = _dot3(ca, _split2(w_ref[...])) + b_ref[...]


def _modulation(c_pad, w, b):
    rows, d = c_pad.shape
    n = w.shape[1]
    return pl.pallas_call(
        _mod_kernel,
        grid=(n // MOD_COLS,),
        in_specs=[
            pl.BlockSpec((rows, d), lambda j: (0, 0)),
            pl.BlockSpec((d, MOD_COLS), lambda j: (0, j)),
            pl.BlockSpec((1, MOD_COLS), lambda j: (0, j)),
        ],
        out_specs=pl.BlockSpec((rows, MOD_COLS), lambda j: (0, j)),
        out_shape=jax.ShapeDtypeStruct((rows, n), F32),
        compiler_params=pltpu.CompilerParams(dimension_semantics=("arbitrary",)),
        name="modulation",
    )(c_pad, w, b)


def _shift_rows(u, tail, j):
    r = pltpu.roll(u, j, 0)
    rt = pltpu.roll(tail, j, 0)
    rows = lax.broadcasted_iota(jnp.int32, tail.shape, 0)
    head = jnp.where(rows < j, rt, r[:SUBLANES])
    return jnp.concatenate([head, r[SUBLANES:]], axis=0)


def _causal_conv(u, tail, w):
    k = w.shape[0]
    out = u * w[k - 1:k]
    for j in range(1, k):
        out = out + _shift_rows(u, tail, j) * w[k - 1 - j:k - j]
    return out


def _mixer_in_kernel(x_ref, mod_ref, gmix_ref, w_ref, wab_ref, cwm_ref, cwq_ref, gco_ref, alog_ref, dtb_ref,
                     ones64_ref, ones128_ref,
                     ya_ref, q_ref, k_ref, v_ref, z_ref, gb_ref,
                     tail_u, tail_qkv):
    @pl.when(pl.program_id(1) == 0)
    def _():
        tail_u[...] = jnp.zeros_like(tail_u)
        tail_qkv[...] = jnp.zeros_like(tail_qkv)

    mod = mod_ref[0]
    groups = [slice(r, r + ROWS_PER_GROUP) for r in range(0, x_ref.shape[1], ROWS_PER_GROUP)]
    h = [_adaln(x_ref[0, rows, :], gmix_ref[...], mod[0:1], mod[1:2]).astype(BF16) for rows in groups]

    def proj(r, i):
        return _dot(h[r], w_ref[:, i * D_CONV:(i + 1) * D_CONV])

    order = (3, 4, 5, 1, 2, 0)
    pre = [dict() for _ in groups]
    for i in order:
        for r in range(len(groups)):
            pre[r][i] = proj(r, i)
    ab_all = [_dot(h[r], wab_ref[...]) for r in range(len(groups))]

    cwq = cwq_ref[...]
    tails_qkv = tail_qkv[...]
    tails_u = tail_u[...]
    for r, rows in enumerate(groups):
        act = []
        new_tails = []
        for i in range(3):
            cols = slice(i * DN_WIDTH, (i + 1) * DN_WIDTH)
            act.append(_silu(_causal_conv(pre[r][3 + i], tails_qkv[:, cols], cwq[:, cols])))
            new_tails.append(pre[r][3 + i][-SUBLANES:])
        tails_qkv = jnp.concatenate(new_tails, axis=1)
        v_ref[0, rows, :] = act[2].astype(v_ref.dtype)
        for i, ref in enumerate((q_ref, k_ref)):
            ss = _segment_sum(act[i] * act[i], ones128_ref[...])
            scale = DN_HEAD_DIM ** -0.5 if i == 0 else 1.0
            ref[0, rows, :] = (act[i] * lax.rsqrt(ss + EPS) * scale).astype(ref.dtype)
        z_ref[0, rows, :] = proj(r, 6).astype(z_ref.dtype)

        u = pre[r][1] * pre[r][2]
        ya = pre[r][0] * _causal_conv(u, tails_u, cwm_ref[...])
        tails_u = u[-SUBLANES:]
        ms = _segment_sum(ya * ya, ones64_ref[...]) * (1.0 / CONV_GROUP_DIM)
        ya_ref[0, rows, :] = ((ya * lax.rsqrt(ms + EPS)) * gco_ref[...]).astype(ya_ref.dtype)

        ab = ab_all[r]
        sp_in = ab + dtb_ref[...]
        softplus = jnp.maximum(sp_in, 0.0) + jnp.log1p(jnp.exp(-jnp.abs(sp_in)))
        g = -jnp.exp(alog_ref[...]) * softplus
        row_in_chunk = lax.broadcasted_iota(jnp.int32, ab.shape, 0) & (CHUNK - 1)
        gc = g
        for l in range(N_LEVELS):
            s = 1 << l
            gc = gc + jnp.where(row_in_chunk >= s, pltpu.roll(gc, s, 0), 0.0)
        g_last = jnp.concatenate(
            [jnp.broadcast_to(gc[c * CHUNK + CHUNK - 1:(c + 1) * CHUNK], (CHUNK, LANES))
             for c in range(ab.shape[0] // CHUNK)], axis=0)
        lane = lax.broadcasted_iota(jnp.int32, ab.shape, 1)
        gb_ref[0, rows, :] = jnp.where(
            lane < GB_BETA, gc, jnp.where(
                lane < GB_EXP_GC, _sigmoid(ab), jnp.where(
                    lane < GB_EXP_REST, jnp.exp(gc), jnp.where(
                        lane < GB_EXP_LAST, jnp.exp(g_last - gc), jnp.exp(g_last)))))
    tail_qkv[...] = tails_qkv
    tail_u[...] = tails_u


def _mixer_in(x, mod, g_mix, w_main, w_ab, cw_mix, cw_qkv, g_conv_out, alog_row, dtb_row, ones64, ones128):
    b, s, d = x.shape
    tm = TOKENS_IN
    const = lambda *shape: pl.BlockSpec(shape, lambda bi, si: (0,) * len(shape))
    resident = lambda *shape: pl.BlockSpec(shape, lambda bi, si: (0,) * len(shape), pipeline_mode=pl.Buffered(1))
    tok = lambda width: pl.BlockSpec((1, tm, width), lambda bi, si: (bi, si, 0))
    return pl.pallas_call(
        _mixer_in_kernel,
        grid=(b, s // tm),
        in_specs=[
            tok(d),
            pl.BlockSpec((1, SUBLANES, d), lambda bi, si: (bi, 0, 0)),
            const(1, d),
            resident(*w_main.shape),
            resident(d, LANES),
            const(3, D_CONV),
            const(4, 3 * DN_WIDTH),
            const(1, D_CONV),
            const(1, LANES),
            const(1, LANES),
            const(MXU_COLS, MXU_COLS),
            const(MXU_COLS, MXU_COLS),
        ],
        out_specs=[tok(D_CONV), tok(DN_WIDTH), tok(DN_WIDTH), tok(DN_WIDTH), tok(DN_WIDTH), tok(LANES)],
        out_shape=[
            jax.ShapeDtypeStruct((b, s, D_CONV), BF16),
            jax.ShapeDtypeStruct((b, s, DN_WIDTH), BF16),
            jax.ShapeDtypeStruct((b, s, DN_WIDTH), BF16),
            jax.ShapeDtypeStruct((b, s, DN_WIDTH), BF16),
            jax.ShapeDtypeStruct((b, s, DN_WIDTH), BF16),
            jax.ShapeDtypeStruct((b, s, LANES), F32),
        ],
        scratch_shapes=[pltpu.VMEM((SUBLANES, D_CONV), F32), pltpu.VMEM((SUBLANES, 3 * DN_WIDTH), F32)],
        compiler_params=pltpu.CompilerParams(dimension_semantics=("arbitrary", "arbitrary"),
                                             vmem_limit_bytes=VMEM_LIMIT_BYTES),
        name="mixer_in",
    )(x, mod, g_mix, w_main, w_ab, cw_mix, cw_qkv, g_conv_out, alog_row, dtb_row, ones64, ones128)


MASK_CAUSAL = 0
MASK_STRICT = 1
MASK_LEVEL0 = 2
MASK16_CHUNK0 = MASK_LEVEL0 + N_LEVELS


def _pair_masks():
    i = np.arange(PAIR)[:, None]
    j = np.arange(PAIR)[None, :]
    same_chunk = (i // CHUNK) == (j // CHUNK)
    out = [same_chunk & (j <= i), same_chunk & (j < i)]
    for l in range(N_LEVELS):
        s = 1 << l
        out.append((j < i) & ((i // s) != (j // s)) & ((i // (2 * s)) == (j // (2 * s))))
    out += [np.broadcast_to(j // CHUNK == e, (PAIR, PAIR)) for e in range(2)]
    return np.stack(out).astype(np.float32)


def _prep_front(f, base, q_ref, k_ref, v_ref, gb_ref, masks_ref):
    insts = [(p, hd) for p in range(PAIRS_PER_PASS) for hd in range(DN_HEADS)]
    n = range(len(insts))
    wide = (PAIR, DN_HEAD_DIM)
    rows = [slice(base + p * PAIR, base + (p + 1) * PAIR) for p, _ in insts]
    cols = [slice(hd * DN_HEAD_DIM, (hd + 1) * DN_HEAD_DIM) for _, hd in insts]
    f.update(insts=insts, rows=rows, cols=cols)

    def column(i, offset):
        hd = insts[i][1]
        return jnp.broadcast_to(gb_ref[0, rows[i], offset + hd:offset + hd + 1], wide)

    f["q"] = [q_ref[0, rows[i], cols[i]].astype(F32) for i in n]
    f["k"] = k = [k_ref[0, rows[i], cols[i]].astype(F32) for i in n]
    f["v"] = [v_ref[0, rows[i], cols[i]].astype(F32) for i in n]
    f["beta"] = beta = [column(i, GB_BETA) for i in n]
    f["kb"] = [k[i] * beta[i] for i in n]
    yield
    f["exp_gc"] = [column(i, GB_EXP_GC) for i in n]
    f["exp_rest"] = [column(i, GB_EXP_REST) for i in n]
    yield
    causal = masks_ref[MASK_CAUSAL]
    gc = [column(i, GB_GC) for i in n]
    f["decay"] = [jnp.where(causal > 0, jnp.exp(gc[i] - gc[i].T), 0.0) for i in n]
    yield


def _prep_scores(f):
    n = range(len(f["insts"]))
    kq = [_dot_nt(jnp.concatenate([f["kb"][i], f["q"][i]], axis=0).astype(BF16), f["k"][i].astype(BF16)) for i in n]
    f["low"] = [kq[i][:PAIR] * f["decay"][i] for i in n]
    f["attn"] = [(kq[i][PAIR:] * f["decay"][i]).astype(BF16) for i in n]


def _prep_inverse(f, masks_ref, masks16_ref):
    low = f["low"]
    n = range(len(low))
    eye = masks_ref[MASK_CAUSAL] - masks_ref[MASK_STRICT]
    t = [eye - low[i] * masks_ref[MASK_LEVEL0] for i in n]
    low16 = [low[i].astype(BF16) for i in n]
    for l in range(1, N_LEVELS):
        t16 = [t[i].astype(BF16) for i in n]
        x = [_dot(t16[i], low16[i] * masks16_ref[MASK_LEVEL0 + l]) for i in n]
        t = [t[i] - _dot(x[i].astype(BF16), t16[i]) for i in n]
        yield
    f["t"] = t


def _prep_tail(f, base, chunk0, gb_ref, masks16_ref, at_ref, bt_ref, dec_ref, qeff_ref, oin_ref):
    insts, q, k, v, beta, kb, attn, t, exp_gc, exp_rest = (
        f[name] for name in ("insts", "q", "k", "v", "beta", "kb", "attn", "t", "exp_gc", "exp_rest"))
    n = range(len(insts))
    uw = [_dot(t[i].astype(BF16), jnp.concatenate([v[i] * beta[i], kb[i] * exp_gc[i]], axis=1).astype(BF16))
          for i in n]
    yield
    a_uw = [_dot(attn[i], uw[i].astype(BF16)) for i in n]
    k_dec = [(k[i] * exp_rest[i]).astype(BF16) for i in n]
    yield

    ab = []
    for i in n:
        u_t = uw[i][:, :DN_HEAD_DIM].T.astype(BF16)
        w_t = uw[i][:, DN_HEAD_DIM:].T.astype(BF16)
        parts = []
        for e in range(2):
            parts += [w_t * masks16_ref[MASK16_CHUNK0 + e], u_t * masks16_ref[MASK16_CHUNK0 + e]]
        ab.append(_dot(jnp.concatenate(parts, axis=0), k_dec[i]))
    yield

    for i, (p, hd) in enumerate(insts):
        rows, cols = f["rows"][i], f["cols"][i]
        qeff_ref[0, rows, cols] = (q[i] * exp_gc[i] - a_uw[i][:, DN_HEAD_DIM:]).astype(qeff_ref.dtype)
        oin_ref[0, rows, cols] = a_uw[i][:, :DN_HEAD_DIM].astype(oin_ref.dtype)
        for e in range(2):
            c = chunk0 + 2 * p + e
            last = base + p * PAIR + (e + 1) * CHUNK - 1
            at_ref[0, hd, c] = ab[i][(2 * e) * DN_HEAD_DIM:(2 * e + 1) * DN_HEAD_DIM].astype(at_ref.dtype)
            bt_ref[0, hd, c] = ab[i][(2 * e + 1) * DN_HEAD_DIM:(2 * e + 2) * DN_HEAD_DIM].astype(bt_ref.dtype)
            dec_ref[0, hd, c:c + 1, :] = jnp.broadcast_to(
                gb_ref[0, last:last + 1, GB_EXP_LAST + hd:GB_EXP_LAST + hd + 1], (1, LANES))
    yield


def _run_interleaved(*gens):
    live = list(gens)
    while live:
        live = [g for g in live if next(g, StopIteration) is not StopIteration]


def _delta_prep_kernel(q_ref, k_ref, v_ref, gb_ref, masks_ref, masks16_ref,
                       at_ref, bt_ref, dec_ref, qeff_ref, oin_ref):
    tokens_per_pass = PAIRS_PER_PASS * PAIR
    n_passes = TOKENS_PREP // tokens_per_pass
    outs = (at_ref, bt_ref, dec_ref, qeff_ref, oin_ref)
    fs = [dict() for _ in range(n_passes)]

    def front(i):
        if i >= n_passes:
            return []
        return [_prep_front(fs[i], i * tokens_per_pass, q_ref, k_ref, v_ref, gb_ref, masks_ref)]

    def inverse(i):
        return _prep_inverse(fs[i], masks_ref, masks16_ref)

    def tail(i):
        return _prep_tail(fs[i], i * tokens_per_pass, i * (tokens_per_pass // CHUNK), gb_ref, masks16_ref, *outs)

    _run_interleaved(*front(0))
    _prep_scores(fs[0])
    _run_interleaved(inverse(0), *front(1))
    for i in range(1, n_passes):
        _prep_scores(fs[i])
        _run_interleaved(tail(i - 1), inverse(i), *front(i + 1))
    _run_interleaved(tail(n_passes - 1))


def _delta_prep(q, k, v, gb, masks):
    b, s, _ = q.shape
    tq = TOKENS_PREP
    cpb = tq // CHUNK
    nc = s // CHUNK
    tok = lambda width: pl.BlockSpec((1, tq, width), lambda bi, si: (bi, si, 0))
    mat = pl.BlockSpec((1, DN_HEADS, cpb, DN_HEAD_DIM, DN_HEAD_DIM), lambda bi, si: (bi, 0, si, 0, 0))
    return pl.pallas_call(
        _delta_prep_kernel,
        grid=(b, s // tq),
        in_specs=[tok(DN_WIDTH), tok(DN_WIDTH), tok(DN_WIDTH), tok(LANES),
                  pl.BlockSpec(masks.shape, lambda bi, si: (0, 0, 0)),
                  pl.BlockSpec(masks.shape, lambda bi, si: (0, 0, 0))],
        out_specs=[mat, mat,
                   pl.BlockSpec((1, DN_HEADS, cpb, LANES), lambda bi, si: (bi, 0, si, 0)),
                   tok(DN_WIDTH), tok(DN_WIDTH)],
        out_shape=[
            jax.ShapeDtypeStruct((b, DN_HEADS, nc, DN_HEAD_DIM, DN_HEAD_DIM), BF16),
            jax.ShapeDtypeStruct((b, DN_HEADS, nc, DN_HEAD_DIM, DN_HEAD_DIM), BF16),
            jax.ShapeDtypeStruct((b, DN_HEADS, nc, LANES), F32),
            jax.ShapeDtypeStruct((b, s, DN_WIDTH), BF16),
            jax.ShapeDtypeStruct((b, s, DN_WIDTH), BF16),
        ],
        compiler_params=pltpu.CompilerParams(dimension_semantics=("arbitrary", "arbitrary"),
                                             vmem_limit_bytes=VMEM_LIMIT_BYTES),
        name="delta_prep",
    )(q, k, v, gb, masks, masks.astype(BF16))


def _delta_scan_kernel(at_ref, bt_ref, dec_ref, qeff_ref, oin_ref, o_ref, state_ref):
    step = pl.program_id(0)

    @pl.when(step == 0)
    def _():
        state_ref[...] = jnp.zeros_like(state_ref)

    nb = at_ref.shape[0]
    chains = [(bi, hd) for bi in range(nb) for hd in range(DN_HEADS)]
    states = [state_ref[bi, hd] for bi, hd in chains]
    for c in range(CHUNKS_SCAN):
        rows = slice(c * CHUNK, (c + 1) * CHUNK)
        for i, (bi, hd) in enumerate(chains):
            cols = slice(hd * DN_HEAD_DIM, (hd + 1) * DN_HEAD_DIM)
            st16 = states[i].astype(BF16)
            o = oin_ref[bi, rows, cols].astype(F32) + _dot_nt(qeff_ref[bi, rows, cols], st16)
            o_ref[bi, rows, cols] = o.astype(o_ref.dtype)
            dec = dec_ref[bi, hd, pl.ds(step * CHUNKS_SCAN + c, 1), :]
            states[i] = states[i] * dec - _dot(st16, at_ref[bi, hd, c]) + bt_ref[bi, hd, c].astype(F32)
    for i, (bi, hd) in enumerate(chains):
        state_ref[bi, hd] = states[i]


def _delta_scan(at, bt, dec, qeff, oin):
    b, s, _ = qeff.shape
    nc = s // CHUNK
    tq = CHUNKS_SCAN * CHUNK
    mat = pl.BlockSpec((b, DN_HEADS, CHUNKS_SCAN, DN_HEAD_DIM, DN_HEAD_DIM), lambda si: (0, 0, si, 0, 0))
    tok = pl.BlockSpec((b, tq, DN_WIDTH), lambda si: (0, si, 0))
    return pl.pallas_call(
        _delta_scan_kernel,
        grid=(nc // CHUNKS_SCAN,),
        in_specs=[mat, mat, pl.BlockSpec((b, DN_HEADS, nc, LANES), lambda si: (0, 0, 0, 0)), tok, tok],
        out_specs=tok,
        out_shape=jax.ShapeDtypeStruct((b, s, DN_WIDTH), BF16),
        scratch_shapes=[pltpu.VMEM((b, DN_HEADS, DN_HEAD_DIM, DN_HEAD_DIM), F32)],
        compiler_params=pltpu.CompilerParams(dimension_semantics=("arbitrary",),
                                             vmem_limit_bytes=VMEM_LIMIT_BYTES),
        name="delta_scan",
    )(at, bt, dec, qeff, oin)


def _mixer_out_kernel(x_ref, ya_ref, o_ref, z_ref, mod_ref, gdn_ref, ones128_ref, wout_ref, gffn_ref, wgu_ref,
                      wd_ref, gfin_ref, out_ref):
    mod = mod_ref[0]
    groups = [slice(r, r + ROWS_PER_GROUP) for r in range(0, x_ref.shape[1], ROWS_PER_GROUP)]
    n = range(len(groups))

    y_a = [_dot(ya_ref[0, rows, :], wout_ref[:D_CONV, :]) for rows in groups]
    x1 = []
    for r, rows in enumerate(groups):
        o = o_ref[0, rows, :].astype(F32)
        ms = _segment_sum(o * o, ones128_ref[...]) * (1.0 / DN_HEAD_DIM)
        yb = ((o * lax.rsqrt(ms + EPS)) * gdn_ref[...]) * _silu(z_ref[0, rows, :].astype(F32))
        y = y_a[r] + _dot(yb.astype(BF16), wout_ref[D_CONV:, :])
        x1.append(x_ref[0, rows, :] + mod[2:3] * y)

    h = [_adaln(x1[r], gffn_ref[...], mod[3:4], mod[4:5]).astype(BF16) for r in n]
    acc = [None for _ in n]
    for j in range(FFN_HIDDEN // FFN_COLS):
        for r in n:
            gate = _dot(h[r], wgu_ref[:, j * FFN_COLS:(j + 1) * FFN_COLS])
            up = _dot(h[r], wgu_ref[:, FFN_HIDDEN + j * FFN_COLS:FFN_HIDDEN + (j + 1) * FFN_COLS])
            part = _dot((_silu(gate) * up).astype(BF16), wd_ref[j * FFN_COLS:(j + 1) * FFN_COLS, :])
            acc[r] = part if acc[r] is None else acc[r] + part
    for r, rows in enumerate(groups):
        out_ref[0, rows, :] = _adaln(x1[r] + mod[5:6] * acc[r], gfin_ref[...], mod[6:7], mod[7:8])


def _mixer_out(x, ya, o, z, mod, g_dn, ones128, w_out, g_ffn, w_gate_up, w_down, g_final):
    b, s, d = x.shape
    tm = TOKENS_OUT
    const = lambda *shape: pl.BlockSpec(shape, lambda bi, si: (0,) * len(shape))
    resident = lambda *shape: pl.BlockSpec(shape, lambda bi, si: (0,) * len(shape), pipeline_mode=pl.Buffered(1))
    tok = lambda width: pl.BlockSpec((1, tm, width), lambda bi, si: (bi, si, 0))
    return pl.pallas_call(
        _mixer_out_kernel,
        grid=(b, s // tm),
        in_specs=[
            tok(d), tok(D_CONV), tok(DN_WIDTH), tok(DN_WIDTH),
            pl.BlockSpec((1, SUBLANES, d), lambda bi, si: (bi, 0, 0)),
            const(1, DN_WIDTH),
            const(MXU_COLS, MXU_COLS),
            resident(d, d),
            const(1, d),
            resident(d, 2 * FFN_HIDDEN),
            resident(FFN_HIDDEN, d),
            const(1, d),
        ],
        out_specs=tok(d),
        out_shape=jax.ShapeDtypeStruct((b, s, d), F32),
        compiler_params=pltpu.CompilerParams(dimension_semantics=("arbitrary", "arbitrary"),
                                             vmem_limit_bytes=VMEM_LIMIT_BYTES),
        name="mixer_out",
    )(x, ya, o, z, mod, g_dn, ones128, w_out, g_ffn, w_gate_up, w_down, g_final)


def _block_ones(n, block):
    i = np.arange(n)
    return jnp.asarray((i[:, None] // block) == (i[None, :] // block), dtype=BF16)


def _gb_lanes(decay_cols, beta_cols):
    groups = {GB_GC: decay_cols, GB_BETA: beta_cols, GB_EXP_GC: decay_cols, GB_EXP_REST: decay_cols,
              GB_EXP_LAST: decay_cols}
    used = [groups[off] for off in sorted(groups)]
    lead = decay_cols.shape[:-1]
    return jnp.concatenate(used + [jnp.zeros(lead + (LANES - len(used) * DN_HEADS,), decay_cols.dtype)], axis=-1)


def kernel(x, c, w_ada, b_ada, w_ada_final, b_ada_final, g_norm_mix, g_norm_ffn, g_norm_final, w_in, conv_w_mix,
           conv_w_qkv, a_log, dt_bias, g_conv_out, g_dn_out, w_out, w_gate_up, w_down):
    b, s, d = x.shape
    assert d == D_MODEL and w_ada.shape[0] == 1, "one layer of width D_MODEL"
    assert s % TOKENS_IN == 0 and s % TOKENS_PREP == 0 and s % TOKENS_OUT == 0
    assert TOKENS_PREP % (SUBLANES * CHUNK) == 0 and (s // CHUNK) % CHUNKS_SCAN == 0

    c_pad = jnp.pad(c, ((0, (-b) % SUBLANES), (0, 0)))
    mod = jnp.concatenate([
        _modulation(c_pad, w_ada[0], b_ada[0].reshape(1, -1)),
        _modulation(c_pad, w_ada_final, b_ada_final.reshape(1, -1)),
    ], axis=1)[:b].reshape(b, SUBLANES, d)

    w_main = w_in[0].astype(BF16)
    w_ab = _gb_lanes(w_main[:, N_MAIN:N_MAIN + DN_HEADS], w_main[:, N_MAIN + DN_HEADS:])
    no_beta = jnp.zeros((1, DN_HEADS), F32)
    alog_row = _gb_lanes(a_log[0].astype(F32).reshape(1, DN_HEADS), no_beta)
    dtb_row = _gb_lanes(dt_bias[0].astype(F32).reshape(1, DN_HEADS), no_beta)
    ones64 = _block_ones(MXU_COLS, CONV_GROUP_DIM)
    ones128 = _block_ones(MXU_COLS, DN_HEAD_DIM)

    ya, q, k, v, z, gb = _mixer_in(
        x, mod, g_norm_mix[0].reshape(1, d), w_main, w_ab, conv_w_mix[0], conv_w_qkv[0],
        g_conv_out[0].reshape(1, D_CONV), alog_row, dtb_row, ones64, ones128)

    o = _delta_scan(*_delta_prep(q, k, v, gb, jnp.asarray(_pair_masks())))

    return _mixer_out(
        x, ya, o, z, mod, jnp.tile(g_dn_out[0], DN_HEADS).reshape(1, DN_WIDTH), ones128,
        w_out[0].astype(BF16), g_norm_ffn[0].reshape(1, d), w_gate_up[0].astype(BF16), w_down[0].astype(BF16),
        g_norm_final.reshape(1, d))
```
